```python
import jax, jax.numpy as jnp
from jax import lax
import numpy as np

D_MODEL = 4096
BATCH = 8
SEQ = 2048
DEPTH = 4
DEC_BATCH = 8
DEC_SEQ = 16
PAST_LEN = 4096

CHUNK = 64
N_MIXERS = 3
N_A = (DEPTH + 2) // 3
N_B = (DEPTH + 1) // 3
N_C = DEPTH // 3
SG_LEN = 128
SG_HEADS = 8
SG_WIDTH = D_MODEL
SG_HD = SG_WIDTH // SG_HEADS
GLA_HEADS = 4
GLA_DK = D_MODEL // 2
GLA_DV = D_MODEL
GLA_HDK = GLA_DK // GLA_HEADS
GLA_HDV = GLA_DV // GLA_HEADS
GLA_RANK = 16
GLA_TAU = 16.0
GLA_BLOCK = 64
CONV_W = 3
MEM_LEN = 256
MEM_HEADS = 4
MEM_HD = D_MODEL // MEM_HEADS
D_FF = 11008
EPS = 1e-6

kernel_name = 'hybrid_streaming_encoder_step'


def rmsnorm(x, g):
    xf = x.astype(jnp.float32)
    y = xf * lax.rsqrt(jnp.mean(xf * xf, axis=-1, keepdims=True) + EPS)
    return (y * g.astype(jnp.float32)).astype(x.dtype)


def layernorm(x, g):
    xf = x.astype(jnp.float32)
    xc = xf - jnp.mean(xf, axis=-1, keepdims=True)
    y = xc * lax.rsqrt(jnp.mean(xc * xc, axis=-1, keepdims=True) + EPS)
    return (y * g.astype(jnp.float32)).astype(x.dtype)


def causal_conv(z, prefix, w):
    width = w.shape[0]
    t = z.shape[1]
    zp = jnp.concatenate([prefix.astype(z.dtype), z], axis=1)
    y = zp[:, 0:t] * w[0]
    for i in range(1, width):
        y = y + zp[:, i:i + t] * w[i]
    return y, zp[:, -(width - 1):]


def spatial_gating_mixer(h, w_in, ln_g, w_s, b_s, w_out):
    b, t, _ = h.shape
    z = jax.nn.gelu(h @ w_in)
    u, v = jnp.split(z, 2, axis=-1)
    v = layernorm(v, ln_g)
    n = -(-t // SG_LEN)
    vp = jnp.pad(v, ((0, 0), (0, n * SG_LEN - t), (0, 0))).reshape(b, n, SG_LEN, SG_HEADS, SG_HD)
    pos = jnp.arange(SG_LEN) // CHUNK
    mask = pos[None, :] <= pos[:, None]
    w = jnp.where(mask, w_s, jnp.zeros_like(w_s))
    mixed = jnp.einsum('hij,bnjhd->bnihd', w, vp) + b_s.T[None, None, :, :, None]
    mixed = mixed.reshape(b, n * SG_LEN, SG_WIDTH)[:, :t]
    return (u * mixed) @ w_out, v


def gla_recurrence(q, k, v, log_a, s0):
    b, t, hh, dk = q.shape
    dv = v.shape[-1]
    blk = GLA_BLOCK if t % GLA_BLOCK == 0 else t
    n = t // blk
    q, k, v, log_a = [a.reshape(b, n, blk, hh, a.shape[-1]) for a in (q, k, v, log_a)]
    g = jnp.cumsum(log_a, axis=2)
    g_last = g[:, :, -1]
    q_t = q * jnp.exp(g)
    k_t = k * jnp.exp(-g)
    k_end = k * jnp.exp(g_last[:, :, None] - g)
    causal = jnp.tril(jnp.ones((blk, blk), dtype=bool))
    att = jnp.einsum('bnihd,bnjhd->bnhij', q_t, k_t)
    att = jnp.where(causal, att, jnp.zeros_like(att))
    o_intra = jnp.einsum('bnhij,bnjhe->bnihe', att, v)

    def step(s, xs):
        qb, kb, vb, gl = xs
        o = jnp.einsum('bihd,bhde->bihe', qb, s)
        s = jnp.exp(gl)[..., None] * s + jnp.einsum('bjhd,bjhe->bhde', kb, vb)
        return s, o

    xs = (jnp.moveaxis(q_t, 1, 0), jnp.moveaxis(k_end, 1, 0), jnp.moveaxis(v, 1, 0), jnp.moveaxis(g_last, 1, 0))
    s_fin, o_inter = lax.scan(step, s0, xs)
    o = o_intra + jnp.moveaxis(o_inter, 0, 1)
    return o.reshape(b, t, hh, dv), s_fin


def gla_mixer(h, s0, w_in, w_gate, b_gate, norm_g, w_out):
    b, t, _ = h.shape
    q, k, v, r, g_lr = jnp.split(h @ w_in, [GLA_DK, 2 * GLA_DK, 2 * GLA_DK + GLA_DV, 2 * GLA_DK + 2 * GLA_DV], axis=-1)
    log_a = jax.nn.log_sigmoid((g_lr @ w_gate + b_gate).astype(jnp.float32)) / GLA_TAU
    heads = lambda a, d: a.astype(jnp.float32).reshape(b, t, GLA_HEADS, d)
    o, s_new = gla_recurrence(heads(q, GLA_HDK) * (GLA_HDK ** -0.5), heads(k, GLA_HDK), heads(v, GLA_HDV),
                              log_a.reshape(b, t, GLA_HEADS, GLA_HDK), s0.astype(jnp.float32))
    o = rmsnorm(o, norm_g).reshape(b, t, GLA_DV).astype(h.dtype) * jax.nn.silu(r)
    return o @ w_out, s_new


def short_conv_mixer(h, prefix, w_in, conv_w, w_out):
    gb, gc, hx = jnp.split(h @ w_in, 3, axis=-1)
    y, buf = causal_conv(gc * hx, prefix, conv_w)
    return (gb * y) @ w_out, buf


def mem_kv(mem, g, w_kv):
    b, m, _ = mem.shape
    k, v = jnp.split(rmsnorm(mem, g) @ w_kv, 2, axis=-1)
    return k.reshape(b, m, MEM_HEADS, MEM_HD), v.reshape(b, m, MEM_HEADS, MEM_HD)


def mem_attention(h, k, v, w_q, w_o):
    b, t, _ = h.shape
    q = (h @ w_q).reshape(b, t, MEM_HEADS, MEM_HD)
    s = jnp.einsum('bthd,bmhd->bhtm', q.astype(jnp.float32), k.astype(jnp.float32)) * (MEM_HD ** -0.5)
    p = jax.nn.softmax(s, axis=-1).astype(v.dtype)
    o = jnp.einsum('bhtm,bmhd->bthd', p, v).reshape(b, t, MEM_HEADS * MEM_HD)
    return o @ w_o


def conv_ffn(h, prefix, w_up, conv_w, w_down):
    a, u = jnp.split(h @ w_up, 2, axis=-1)
    a, buf = causal_conv(a, prefix, conv_w)
    return (jax.nn.silu(a) * u) @ w_down, buf


def setup_inputs(seed: int = 0) -> dict:
    key = jax.random.key(seed)
    ks = iter(jax.random.split(key, 40))
    nrm = lambda shape, scale=1.0: jax.random.normal(next(ks), shape, jnp.float32) * scale
    gain = lambda shape: 1.0 + 0.05 * jax.random.normal(next(ks), shape, jnp.float32)
    D = D_MODEL
    return {
        'x_prompt': nrm((BATCH, SEQ, D)),
        'x_sample': nrm((DEC_BATCH, DEC_SEQ, D)),
        'mem_prompt': nrm((BATCH, MEM_LEN, D)),
        'state_gla': nrm((N_B, DEC_BATCH, GLA_HEADS, GLA_HDK, GLA_HDV)),
        'cache_conv_c': nrm((N_C, DEC_BATCH, CONV_W - 1, D)),
        'cache_ffn_conv': nrm((DEPTH, DEC_BATCH, 2, D_FF)),
        'cache_mem_k': nrm((DEPTH, DEC_BATCH, MEM_LEN, MEM_HEADS, MEM_HD)),
        'cache_mem_v': nrm((DEPTH, DEC_BATCH, MEM_LEN, MEM_HEADS, MEM_HD)),
        'norm_g': gain((DEPTH, 6, D)),
        'mem_norm_g': gain((DEPTH, D)),
        'final_g': gain((D,)),
        'sg_w_in': nrm((N_A, D, 2 * SG_WIDTH), D ** -0.5),
        'sg_ln_g': gain((N_A, SG_WIDTH)),
        'sg_w_s': nrm((N_A, SG_HEADS, SG_LEN, SG_LEN), SG_LEN ** -0.5),
        'sg_b_s': 1.0 + nrm((N_A, SG_HEADS, SG_LEN), 0.1),
        'sg_w_out': nrm((N_A, SG_WIDTH, D), SG_WIDTH ** -0.5),
        'gla_w_in': nrm((N_B, D, 2 * GLA_DK + 2 * GLA_DV + GLA_RANK), D ** -0.5),
        'gla_w_gate': nrm((N_B, GLA_RANK, GLA_DK), GLA_RANK ** -0.5),
        'gla_b_gate': nrm((N_B, GLA_DK), 0.1),
        'gla_norm_g': gain((N_B, GLA_HDV)),
        'gla_w_out': nrm((N_B, GLA_DV, D), GLA_DV ** -0.5),
        'sc_w_in': nrm((N_C, D, 3 * D), D ** -0.5),
        'sc_conv': nrm((N_C, CONV_W, D), CONV_W ** -0.5),
        'sc_w_out': nrm((N_C, D, D), D ** -0.5),
        'mem_w_q': nrm((DEPTH, D, MEM_HEADS * MEM_HD), D ** -0.5),
        'mem_w_kv': nrm((DEPTH, D, 2 * MEM_HEADS * MEM_HD), D ** -0.5),
        'mem_w_o': nrm((DEPTH, MEM_HEADS * MEM_HD, D), (MEM_HEADS * MEM_HD) ** -0.5),
        'ffn_w_up': nrm((DEPTH, D, 2 * D_FF), D ** -0.5),
        'ffn_conv': nrm((DEPTH, 3, D_FF), 3 ** -0.5),
        'ffn_w_down': nrm((DEPTH, D_FF, D), D_FF ** -0.5),
    }


def reference(x_prompt, x_sample, mem_prompt, state_gla, cache_conv_c, cache_ffn_conv, cache_mem_k, cache_mem_v,
              norm_g, mem_norm_g, final_g,
              sg_w_in, sg_ln_g, sg_w_s, sg_b_s, sg_w_out,
              gla_w_in, gla_w_gate, gla_b_gate, gla_norm_g, gla_w_out,
              sc_w_in, sc_conv, sc_w_out,
              mem_w_q, mem_w_kv, mem_w_o,
              ffn_w_up, ffn_conv, ffn_w_down):

    def run(x, mk, mv, s_gla, buf_c, buf_f):
        sg_v, gla_out, c_out, f_out = [], [], [], []
        ia = ib = ic = 0
        for l in range(DEPTH):
            h = rmsnorm(x, norm_g[l, 0])
            kind = l % N_MIXERS
            if kind == 0:
                h, v_rows = spatial_gating_mixer(h, sg_w_in[ia], sg_ln_g[ia], sg_w_s[ia], sg_b_s[ia], sg_w_out[ia])
                sg_v.append(v_rows)
                ia += 1
            elif kind == 1:
                h, s_new = gla_mixer(h, s_gla[ib], gla_w_in[ib], gla_w_gate[ib], gla_b_gate[ib], gla_norm_g[ib], gla_w_out[ib])
                gla_out.append(s_new)
                ib += 1
            else:
                h, cb = short_conv_mixer(h, buf_c[ic], sc_w_in[ic], sc_conv[ic], sc_w_out[ic])
                c_out.append(cb)
                ic += 1
            x = x + rmsnorm(h, norm_g[l, 1])
            h = mem_attention(rmsnorm(x, norm_g[l, 2]), mk[l], mv[l], mem_w_q[l], mem_w_o[l])
            x = x + rmsnorm(h, norm_g[l, 3])
            h, fb = conv_ffn(rmsnorm(x, norm_g[l, 4]), buf_f[l], ffn_w_up[l], ffn_conv[l], ffn_w_down[l])
            f_out.append(fb)
            x = x + rmsnorm(h, norm_g[l, 5])
        return rmsnorm(x, final_g), jnp.stack(sg_v), jnp.stack(gla_out), jnp.stack(c_out), jnp.stack(f_out)

    bp = x_prompt.shape[0]
    kvs = [mem_kv(mem_prompt, mem_norm_g[l], mem_w_kv[l]) for l in range(DEPTH)]
    mem_k_p = jnp.stack([kv[0] for kv in kvs])
    mem_v_p = jnp.stack([kv[1] for kv in kvs])
    y_p, _, gla_p, conv_p, ffn_p = run(
        x_prompt, mem_k_p, mem_v_p,
        jnp.zeros((N_B, bp, GLA_HEADS, GLA_HDK, GLA_HDV), jnp.float32),
        jnp.zeros((N_C, bp, CONV_W - 1, D_MODEL), x_prompt.dtype),
        jnp.zeros((DEPTH, bp, 2, D_FF), x_prompt.dtype))
    y_s, sgv_s, gla_s, conv_s, ffn_s = run(x_sample, cache_mem_k, cache_mem_v, state_gla, cache_conv_c, cache_ffn_conv)
    return (y_p, y_s, gla_p, conv_p, ffn_p, mem_k_p, mem_v_p, sgv_s, gla_s, conv_s, ffn_s)
```

```python
import functools

import jax
import jax.numpy as jnp
from jax import lax
from jax.experimental import pallas as pl
from jax.experimental.pallas import tpu as pltpu

F32 = jnp.float32
BF16 = jnp.bfloat16

EPS = 1e-6
CHUNK = 64
GLA_TAU = 16.0
GLA_BLOCK = 64
N_MIXERS = 3

V7X_VMEM_LIMIT_BYTES = 56 * 1024 * 1024
LANES = 128
MXU_DIM = 256
FFN_PAD = 1024


def _cparams(n_axes):
    return pltpu.CompilerParams(dimension_semantics=("arbitrary",) * n_axes,
                                vmem_limit_bytes=V7X_VMEM_LIMIT_BYTES)


def _dot(a, b):
    return jnp.dot(a, b, preferred_element_type=F32)


def _rms(x):
    return x * lax.rsqrt(jnp.mean(x * x, axis=-1, keepdims=True) + EPS)


def _sigmoid(x):
    return 1.0 / (1.0 + jnp.exp(-x))


def _rms_cast_kernel(x_ref, g_ref, o_ref):
    o_ref[...] = (_rms(x_ref[...]) * g_ref[...]).astype(o_ref.dtype)


def _rms_cast(x, g, *, tm):
    m, d = x.shape
    return pl.pallas_call(
        _rms_cast_kernel,
        grid=(m // tm,),
        in_specs=[pl.BlockSpec((tm, d), lambda i: (i, 0)), pl.BlockSpec((1, d), lambda i: (0, 0))],
        out_specs=pl.BlockSpec((tm, d), lambda i: (i, 0)),
        out_shape=jax.ShapeDtypeStruct((m, d), BF16),
        compiler_params=_cparams(1),
        name="rms_cast",
    )(x, g)


def _proj_kernel(h_ref, w_ref, o_ref, *, act):
    acc = _dot(h_ref[...], w_ref[...])
    if act == "gelu":
        acc = 0.5 * acc * (1.0 + jnp.tanh(0.7978845608028654 * (acc + 0.044715 * (acc * acc * acc))))
    o_ref[...] = acc.astype(o_ref.dtype)


def _proj(h, w, *, act=None, tm, tn):
    m, k = h.shape
    n = w.shape[1]
    return pl.pallas_call(
        functools.partial(_proj_kernel, act=act),
        grid=(n // tn, m // tm),
        in_specs=[pl.BlockSpec((tm, k), lambda j, i: (i, 0)), pl.BlockSpec((k, tn), lambda j, i: (0, j))],
        out_specs=pl.BlockSpec((tm, tn), lambda j, i: (i, j)),
        out_shape=jax.ShapeDtypeStruct((m, n), BF16),
        compiler_params=_cparams(2),
        name="proj_" + (act or "plain"),
    )(h, w)


def _pcg_kernel(*refs, mode, tm, seq, nseg):
    if mode == "ffn":
        h_ref, wz_ref, wg_ref, cw_ref, pre_ref, o_ref, cache_ref, scr = refs
    else:
        h_ref, wg_ref, wz_ref, wx_ref, cw_ref, pre_ref, o_ref, cache_ref, scr = refs
    i = pl.program_id(1)
    hv = h_ref[...]
    zc = _dot(hv, wz_ref[...])
    if mode == "sc":
        zc = zc * _dot(hv, wx_ref[...])
    gate = _dot(hv, wg_ref[...])
    cw = cw_ref[...]

    if nseg == 1:
        first = (i % (seq // tm)) == 0

        @pl.when(first)
        def _():
            scr[6:8, :] = pre_ref[0]

        @pl.when(jnp.logical_not(first))
        def _():
            scr[6:8, :] = scr[tm + 6:tm + 8, :]
    else:
        scr[6:8, :] = pre_ref[0]

    scr[8:8 + tm, :] = zc
    sh1 = scr[7:7 + tm, :]
    sh2 = scr[6:6 + tm, :]
    if nseg == 1:
        cache_ref[0] = scr[tm + 6:tm + 8, :]
    else:
        row = lax.broadcasted_iota(jnp.int32, (tm, 1), 0)
        for s in range(nseg):
            p = pre_ref[s]
            sh1 = jnp.where(row == s * seq, p[1:2, :], sh1)
            sh2 = jnp.where(row == s * seq, p[0:1, :], sh2)
            sh2 = jnp.where(row == s * seq + 1, p[1:2, :], sh2)
            cache_ref[s] = scr[8 + (s + 1) * seq - 2:8 + (s + 1) * seq, :]
    y = cw[0:1, :] * sh2 + cw[1:2, :] * sh1 + cw[2:3, :] * zc
    if mode == "ffn":
        out = (y * _sigmoid(y)) * gate
    else:
        out = gate * y
    o_ref[...] = out.astype(o_ref.dtype)


def _proj_conv_gate(h, w, conv_w, prefix, *, mode, seq, tm, tn):
    m, k = h.shape
    width = conv_w.shape[1]
    nj = width // tn
    assert seq >= 2 and (seq % tm == 0 or tm % seq == 0)
    nseg = max(1, tm // seq)
    if nseg == 1:
        tiles_per_seq = seq // tm
        pre_map = lambda j, i: (i // tiles_per_seq, 0, j)
    else:
        pre_map = lambda j, i: (i, 0, j)
    n_groups = 2 if mode == "ffn" else 3
    w_specs = [pl.BlockSpec((k, tn), functools.partial(lambda j, i, g: (0, j + g * nj), g=g)) for g in range(n_groups)]
    nbatch = m // seq
    out, cache = pl.pallas_call(
        functools.partial(_pcg_kernel, mode=mode, tm=tm, seq=seq, nseg=nseg),
        grid=(nj, m // tm),
        in_specs=[pl.BlockSpec((tm, k), lambda j, i: (i, 0))] + w_specs + [
            pl.BlockSpec((3, tn), lambda j, i: (0, j)),
            pl.BlockSpec((nseg, 2, tn), pre_map),
        ],
        out_specs=[pl.BlockSpec((tm, tn), lambda j, i: (i, j)), pl.BlockSpec((nseg, 2, tn), pre_map)],
        out_shape=[jax.ShapeDtypeStruct((m, width), BF16), jax.ShapeDtypeStruct((nbatch, 2, width), F32)],
        scratch_shapes=[pltpu.VMEM((tm + 8, tn), F32)],
        compiler_params=_cparams(2),
        name="proj_conv_gate_" + mode,
    )(h, *([w] * n_groups), conv_w, prefix)
    return out, cache


def _sg_mix_kernel(zu_ref, zv_ref, lng_ref, wm_ref, bias_ref, o_ref, *v_refs, heads):
    vpre = zv_ref[...].astype(F32)
    xc = vpre - jnp.mean(vpre, axis=-1, keepdims=True)
    v = xc * lax.rsqrt(jnp.mean(xc * xc, axis=-1, keepdims=True) + EPS) * lng_ref[...]
    if v_refs:
        v_refs[0][...] = v
    vb = v.astype(BF16)
    hd = vb.shape[1] // heads
    bias = bias_ref[...]
    for h in range(heads):
        mixed = _dot(wm_ref[h], vb[:, h * hd:(h + 1) * hd]) + bias[:, h:h + 1]
        o_ref[:, h * hd:(h + 1) * hd] = (zu_ref[:, h * hd:(h + 1) * hd].astype(F32) * mixed).astype(o_ref.dtype)


def _sg_mix(z, ln_g, wm, bias, *, emit_v):
    m = z.shape[0]
    width = z.shape[1] // 2
    heads, ln = wm.shape[0], wm.shape[1]
    out_shape = [jax.ShapeDtypeStruct((m, width), BF16)]
    out_specs = [pl.BlockSpec((ln, width), lambda c: (c, 0))]
    if emit_v:
        out_shape.append(jax.ShapeDtypeStruct((m, width), F32))
        out_specs.append(pl.BlockSpec((ln, width), lambda c: (c, 0)))
    res = pl.pallas_call(
        functools.partial(_sg_mix_kernel, heads=heads),
        grid=(m // ln,),
        in_specs=[
            pl.BlockSpec((ln, width), lambda c: (c, 0)),
            pl.BlockSpec((ln, width), lambda c: (c, 1)),
            pl.BlockSpec((1, width), lambda c: (0, 0)),
            pl.BlockSpec((heads, ln, ln), lambda c: (0, 0, 0)),
            pl.BlockSpec((ln, heads), lambda c: (0, 0)),
        ],
        out_specs=out_specs,
        out_shape=out_shape,
        compiler_params=_cparams(1),
        name="sg_mix",
    )(z, z, ln_g, wm, bias)
    return (res[0], res[1]) if emit_v else (res[0], None)


def _gla_kernel(q_ref, k_ref, v_ref, r_ref, glr_ref, wg_ref, bg_ref, ng_ref, s0_ref, o_ref, sout_ref, s_scr,
                *, blk, nblk, scale):
    n = pl.program_id(2)

    @pl.when(n == 0)
    def _():
        s_scr[...] = s0_ref[...]

    pre = _dot(glr_ref[...], wg_ref[...]) + bg_ref[...]
    la = (jnp.minimum(pre, 0.0) - jnp.log1p(jnp.exp(-jnp.abs(pre)))) * (1.0 / GLA_TAU)
    ri = lax.broadcasted_iota(jnp.int32, (blk, blk), 0)
    ci = lax.broadcasted_iota(jnp.int32, (blk, blk), 1)
    causal = ri >= ci
    tri = jnp.where(causal, 1.0, 0.0).astype(BF16)
    la1 = la.astype(BF16)
    rem = la - la1.astype(F32)
    la2 = rem.astype(BF16)
    la3 = (rem - la2.astype(F32)).astype(BF16)
    g = _dot(tri, la1) + _dot(tri, la2) + _dot(tri, la3)
    g_last = g[blk - 1:blk, :]
    q = q_ref[...].astype(F32) * scale
    k = k_ref[...].astype(F32)
    q_t = (q * jnp.exp(g)).astype(BF16)
    k_t = (k * jnp.exp(-g)).astype(BF16)
    k_end = (k * jnp.exp(g_last - g)).astype(BF16)
    att = lax.dot_general(q_t, k_t, (((1,), (1,)), ((), ())), preferred_element_type=F32)
    att = jnp.where(causal, att, 0.0)
    v = v_ref[...]
    s = s_scr[...]
    o = _dot(att.astype(BF16), v) + _dot(q_t, s.astype(BF16))
    decay = jnp.transpose(jnp.broadcast_to(jnp.exp(g_last), (LANES, g.shape[1])))[:, 0:1]
    s_new = decay * s + lax.dot_general(k_end, v, (((0,), (0,)), ((), ())), preferred_element_type=F32)
    s_scr[...] = s_new

    @pl.when(n == nblk - 1)
    def _():
        sout_ref[...] = s_new

    r = r_ref[...].astype(F32)
    o_ref[...] = (_rms(o) * ng_ref[...] * (r * _sigmoid(r))).astype(o_ref.dtype)


def _gla_core(proj, glr, w_gate, b_gate, norm_g, s0, *, seq, heads):
    m = proj.shape[0]
    nbatch = m // seq
    dk, dv = s0.shape[2], s0.shape[3]
    blk = GLA_BLOCK if seq % GLA_BLOCK == 0 else seq
    nblk = seq // blk
    kq = heads * dk // dk
    row = lambda b, h, n: b * nblk + n
    out, s_new = pl.pallas_call(
        functools.partial(_gla_kernel, blk=blk, nblk=nblk, scale=float(dk) ** -0.5),
        grid=(nbatch, heads, nblk),
        in_specs=[
            pl.BlockSpec((blk, dk), lambda b, h, n: (row(b, h, n), h)),
            pl.BlockSpec((blk, dk), lambda b, h, n: (row(b, h, n), kq + h)),
            pl.BlockSpec((blk, dv), lambda b, h, n: (row(b, h, n), (2 * heads * dk) // dv + h)),
            pl.BlockSpec((blk, dv), lambda b, h, n: (row(b, h, n), (2 * heads * dk) // dv + heads + h)),
            pl.BlockSpec((blk, LANES), lambda b, h, n: (row(b, h, n), 0)),
            pl.BlockSpec((LANES, dk), lambda b, h, n: (0, h)),
            pl.BlockSpec((1, dk), lambda b, h, n: (0, h)),
            pl.BlockSpec((1, dv), lambda b, h, n: (0, 0)),
            pl.BlockSpec((None, None, dk, dv), lambda b, h, n: (b, h, 0, 0)),
        ],
        out_specs=[
            pl.BlockSpec((blk, dv), lambda b, h, n: (row(b, h, n), h)),
            pl.BlockSpec((None, None, dk, dv), lambda b, h, n: (b, h, 0, 0)),
        ],
        out_shape=[jax.ShapeDtypeStruct((m, heads * dv), BF16), jax.ShapeDtypeStruct(s0.shape, F32)],
        scratch_shapes=[pltpu.VMEM((dk, dv), F32)],
        compiler_params=_cparams(3),
        name="gla_core",
    )(proj, proj, proj, proj, glr, w_gate, b_gate, norm_g, s0)
    return out, s_new


def _qattn_kernel(h_ref, wq_ref, k_ref, v_ref, o_ref, q_scr, *, ts, scale):
    s = pl.program_id(2)

    @pl.when(s == 0)
    def _():
        q_scr[...] = _dot(h_ref[...], wq_ref[...]).astype(BF16)

    qs = q_scr[pl.ds(pl.multiple_of(s * ts, ts), ts), :]
    sc = lax.dot_general(qs, k_ref[...].astype(BF16), (((1,), (1,)), ((), ())), preferred_element_type=F32) * scale
    p = jnp.exp(sc - jnp.max(sc, axis=-1, keepdims=True))
    p = p / jnp.sum(p, axis=-1, keepdims=True)
    o_ref[...] = _dot(p.astype(BF16), v_ref[...].astype(BF16)).astype(o_ref.dtype)


def _q_attn(h, wq, mk, mv, layer, *, seq, heads, tm):
    m, k = h.shape
    hd = wq.shape[1] // heads
    mem_len = mk.shape[2]
    ts = min(tm, seq)
    nb = tm // ts
    tiles_per_seq = seq // ts
    kv_map = lambda i, hh, s: (layer, (i * nb + s) // tiles_per_seq, 0, hh)
    return pl.pallas_call(
        functools.partial(_qattn_kernel, ts=ts, scale=float(hd) ** -0.5),
        grid=(m // tm, heads, nb),
        in_specs=[
            pl.BlockSpec((tm, k), lambda i, hh, s: (i, 0)),
            pl.BlockSpec((k, hd), lambda i, hh, s: (0, hh)),
            pl.BlockSpec((None, None, mem_len, hd), kv_map),
            pl.BlockSpec((None, None, mem_len, hd), kv_map),
        ],
        out_specs=pl.BlockSpec((ts, hd), lambda i, hh, s: (i * nb + s, hh)),
        out_shape=jax.ShapeDtypeStruct((m, heads * hd), BF16),
        scratch_shapes=[pltpu.VMEM((tm, hd), BF16)],
        compiler_params=_cparams(3),
        name="q_attn",
    )(h, wq, mk, mv)


def _mem_kv_kernel(mem_ref, g_ref, wk_ref, wv_ref, k_ref, v_ref, hn_scr):
    @pl.when(pl.program_id(2) == 0)
    def _():
        hn_scr[...] = (_rms(mem_ref[...]) * g_ref[...]).astype(BF16)

    hn = hn_scr[...]
    k_ref[...] = _dot(hn, wk_ref[...])
    v_ref[...] = _dot(hn, wv_ref[...])


def _mem_kv(mem, g, w_kv, *, tm, tn):
    m, d = mem.shape
    depth = w_kv.shape[0]
    n = w_kv.shape[2] // 2
    nj = n // tn
    return pl.pallas_call(
        _mem_kv_kernel,
        grid=(depth, m // tm, nj),
        in_specs=[
            pl.BlockSpec((tm, d), lambda l, i, j: (i, 0)),
            pl.BlockSpec((None, 1, d), lambda l, i, j: (l, 0, 0)),
            pl.BlockSpec((None, d, tn), lambda l, i, j: (l, 0, j)),
            pl.BlockSpec((None, d, tn), lambda l, i, j: (l, 0, j + nj)),
        ],
        out_specs=[pl.BlockSpec((None, tm, tn), lambda l, i, j: (l, i, j)),
                   pl.BlockSpec((None, tm, tn), lambda l, i, j: (l, i, j))],
        out_shape=[jax.ShapeDtypeStruct((depth, m, n), F32), jax.ShapeDtypeStruct((depth, m, n), F32)],
        scratch_shapes=[pltpu.VMEM((tm, d), BF16)],
        compiler_params=_cparams(3),
        name="mem_kv",
    )(mem, g, w_kv, w_kv)


def _outproj_kernel(lhs_ref, w_ref, x_ref, gpost_ref, gnext_ref, xo_ref, hn_ref, *, nk, rows):
    kk = pl.program_id(1)
    d = xo_ref.shape[1]
    tn = min(d, 1024)

    def accumulate(first):
        lhs = lhs_ref[...]
        for c in range(d // tn):
            cs = slice(c * tn, (c + 1) * tn)
            part = _dot(lhs, w_ref[:, cs])
            xo_ref[:, cs] = part if first else xo_ref[:, cs] + part

    pl.when(kk == 0)(functools.partial(accumulate, True))
    pl.when(kk > 0)(functools.partial(accumulate, False))

    @pl.when(kk == nk - 1)
    def _():
        gpost = gpost_ref[...]
        gnext = gnext_ref[...]

        def body(c, carry):
            sl = pl.ds(pl.multiple_of(c * rows, rows), rows)
            xn = x_ref[sl, :] + _rms(xo_ref[sl, :]) * gpost
            xo_ref[sl, :] = xn
            hn_ref[sl, :] = (_rms(xn) * gnext).astype(hn_ref.dtype)
            return carry

        lax.fori_loop(0, xo_ref.shape[0] // rows, body, 0)


def _outproj(lhs, w, x, g_post, g_next, *, hn_dtype, tm, tk):
    m, kdim = lhs.shape
    d = w.shape[1]
    nk = kdim // tk
    rows = min(tm, 64)
    return pl.pallas_call(
        functools.partial(_outproj_kernel, nk=nk, rows=rows),
        grid=(m // tm, nk),
        in_specs=[
            pl.BlockSpec((tm, tk), lambda i, kk: (i, kk)),
            pl.BlockSpec((tk, d), lambda i, kk: (kk, 0)),
            pl.BlockSpec((tm, d), lambda i, kk: (i, 0)),
            pl.BlockSpec((1, d), lambda i, kk: (0, 0)),
            pl.BlockSpec((1, d), lambda i, kk: (0, 0)),
        ],
        out_specs=[pl.BlockSpec((tm, d), lambda i, kk: (i, 0)), pl.BlockSpec((tm, d), lambda i, kk: (i, 0))],
        out_shape=[jax.ShapeDtypeStruct((m, d), F32), jax.ShapeDtypeStruct((m, d), hn_dtype)],
        compiler_params=_cparams(2),
        name="outproj",
    )(lhs, w, x, g_post, g_next)


def _pad_last(a, n):
    return jnp.pad(a, [(0, 0)] * (a.ndim - 1) + [(0, n - a.shape[-1])])


def _tiles(m):
    big = m >= 1024
    return dict(
        norm=256 if big else m,
        proj=1024 if big else m,
        sc=512 if big else m,
        attn=512 if big else m,
        out=512 if big else m,
        out_last=256 if big else m,
    )


def _sg_weights(w_s, b_s, seq):
    heads, ln, _ = w_s.shape
    pos = jnp.arange(ln) // CHUNK
    w = jnp.where(pos[None, :] <= pos[:, None], w_s, jnp.zeros_like(w_s))
    if seq % ln == 0:
        return w.astype(BF16), b_s.T
    assert ln % seq == 0
    reps = ln // seq
    blockdiag = jnp.kron(jnp.eye(reps, dtype=w.dtype), jnp.ones((seq, seq), w.dtype))
    wm = jnp.tile(w[:, :seq, :seq], (1, reps, reps)) * blockdiag
    return wm.astype(BF16), jnp.tile(b_s[:, :seq].T, (reps, 1))


def kernel(x_prompt, x_sample, mem_prompt, state_gla, cache_conv_c, cache_ffn_conv, cache_mem_k, cache_mem_v,
           norm_g, mem_norm_g, final_g,
           sg_w_in, sg_ln_g, sg_w_s, sg_b_s, sg_w_out,
           gla_w_in, gla_w_gate, gla_b_gate, gla_norm_g, gla_w_out,
           sc_w_in, sc_conv, sc_w_out,
           mem_w_q, mem_w_kv, mem_w_o,
           ffn_w_up, ffn_conv, ffn_w_down):
    depth = norm_g.shape[0]
    d_model = x_prompt.shape[-1]
    d_ff = ffn_conv.shape[-1]
    d_ffp = -(-d_ff // FFN_PAD) * FFN_PAD
    gla_heads, gla_dk, gla_dv = state_gla.shape[2], state_gla.shape[3], state_gla.shape[4]
    gla_main = 2 * gla_heads * (gla_dk + gla_dv)
    mem_heads = cache_mem_k.shape[3]
    mem_len = cache_mem_k.shape[2]
    bf = lambda a: a.astype(BF16)
    row = lambda a: a.reshape(1, -1)

    w_sg_in = [bf(sg_w_in[a]) for a in range(sg_w_in.shape[0])]
    w_sg_out = [bf(sg_w_out[a]) for a in range(sg_w_out.shape[0])]
    w_gla_in = [bf(gla_w_in[b][:, :gla_main]) for b in range(gla_w_in.shape[0])]
    w_gla_lr = [_pad_last(bf(gla_w_in[b][:, gla_main:]), LANES) for b in range(gla_w_in.shape[0])]
    w_gla_gate = [jnp.pad(bf(gla_w_gate[b]), ((0, LANES - gla_w_gate.shape[1]), (0, 0)))
                  for b in range(gla_w_gate.shape[0])]
    w_gla_out = [bf(gla_w_out[b]) for b in range(gla_w_out.shape[0])]
    w_sc_in = [bf(sc_w_in[c]) for c in range(sc_w_in.shape[0])]
    w_sc_out = [bf(sc_w_out[c]) for c in range(sc_w_out.shape[0])]
    w_q = [bf(mem_w_q[l]) for l in range(depth)]
    w_o = [bf(mem_w_o[l]) for l in range(depth)]
    w_up = [jnp.concatenate([_pad_last(bf(ffn_w_up[l][:, :d_ff]), d_ffp), _pad_last(bf(ffn_w_up[l][:, d_ff:]), d_ffp)],
                            axis=1) for l in range(depth)]
    w_down = [jnp.pad(bf(ffn_w_down[l]), ((0, d_ffp - d_ff), (0, 0))) for l in range(depth)]
    ffn_cw = [_pad_last(ffn_conv[l], d_ffp) for l in range(depth)]

    def run(x, seq, mk, mv, s_gla, buf_c, buf_f, emit_v):
        m = x.shape[0]
        t = _tiles(m)
        sg_v, gla_out, c_out, f_out = [], [], [], []
        ia = ib = ic = 0
        hn = _rms_cast(x, row(norm_g[0, 0]), tm=t["norm"])
        for l in range(depth):
            kind = l % N_MIXERS
            if kind == 0:
                z = _proj(hn, w_sg_in[ia], act="gelu", tm=t["proj"], tn=512)
                wm, bias = _sg_weights(sg_w_s[ia], sg_b_s[ia], seq)
                lhs, v_rows = _sg_mix(z, row(sg_ln_g[ia]), wm, bias, emit_v=emit_v)
                sg_v.append(v_rows)
                w_out = w_sg_out[ia]
                ia += 1
            elif kind == 1:
                proj = _proj(hn, w_gla_in[ib], tm=t["proj"], tn=512)
                glr = _proj(hn, w_gla_lr[ib], tm=t["proj"], tn=LANES)
                lhs, s_new = _gla_core(proj, glr, w_gla_gate[ib], row(gla_b_gate[ib]), row(gla_norm_g[ib]),
                                       s_gla[ib], seq=seq, heads=gla_heads)
                gla_out.append(s_new)
                w_out = w_gla_out[ib]
                ib += 1
            else:
                lhs, cb = _proj_conv_gate(hn, w_sc_in[ic], sc_conv[ic], buf_c[ic], mode="sc", seq=seq,
                                          tm=t["sc"], tn=512)
                c_out.append(cb)
                w_out = w_sc_out[ic]
                ic += 1
            x, hn = _outproj(lhs, w_out, x, row(norm_g[l, 1]), row(norm_g[l, 2]), hn_dtype=BF16, tm=t["out"], tk=512)
            att = _q_attn(hn, w_q[l], mk, mv, l, seq=seq, heads=mem_heads, tm=t["attn"])
            x, hn = _outproj(att, w_o[l], x, row(norm_g[l, 3]), row(norm_g[l, 4]), hn_dtype=BF16, tm=t["out"], tk=512)
            hid, fb = _proj_conv_gate(hn, w_up[l], ffn_cw[l], _pad_last(buf_f[l], d_ffp), mode="ffn", seq=seq,
                                      tm=t["proj"], tn=512)
            f_out.append(fb[:, :, :d_ff])
            last = l == depth - 1
            g_next = final_g if last else norm_g[l + 1, 0]
            x, hn = _outproj(hid, w_down[l], x, row(norm_g[l, 5]), row(g_next),
                             hn_dtype=F32 if last else BF16, tm=t["out_last"] if last else t["out"], tk=512)
        return hn, sg_v, gla_out, c_out, f_out

    bp, sp, _ = x_prompt.shape
    bs, ss, _ = x_sample.shape

    mem2d = mem_prompt.reshape(bp * mem_len, d_model)
    mem_k_p, mem_v_p = _mem_kv(mem2d, mem_norm_g.reshape(depth, 1, d_model), bf(mem_w_kv), tm=512, tn=512)
    mem_k_p = mem_k_p.reshape(depth, bp, mem_len, d_model)
    mem_v_p = mem_v_p.reshape(depth, bp, mem_len, d_model)
    y_p, _, gla_p, conv_p, ffn_p = run(
        x_prompt.reshape(bp * sp, d_model), sp, mem_k_p, mem_v_p,
        jnp.zeros((state_gla.shape[0], bp) + state_gla.shape[2:], F32),
        jnp.zeros((cache_conv_c.shape[0], bp, 2, d_model), F32),
        jnp.zeros((depth, bp, 2, d_ff), F32), emit_v=False)

    y_s, sgv_s, gla_s, conv_s, ffn_s = run(
        x_sample.reshape(bs * ss, d_model), ss,
        cache_mem_k.reshape(depth, bs, mem_len, -1), cache_mem_v.reshape(depth, bs, mem_len, -1),
        state_gla, cache_conv_c, cache_ffn_conv, emit_v=True)

    kv_shape = (depth, bp, mem_len, mem_heads, d_model // mem_heads)
    return (y_p.reshape(bp, sp, d_model), y_s.reshape(bs, ss, d_model),
            jnp.stack(gla_p), jnp.stack(conv_p), jnp.stack(ffn_p),
            mem_k_p.reshape(kv_shape), mem_v_p.reshape(kv_shape),
            jnp.stack([v.reshape(bs, ss, -1) for v in sgv_s]), jnp.stack(gla_s), jnp.stack(conv_s), jnp.stack(ffn_s))
```

```python
import functools

import jax
import jax.numpy as jnp
from jax import lax
from jax.experimental import pallas as pl
from jax.experimental.pallas import tpu as pltpu

F32 = jnp.float32
BF16 = jnp.bfloat16

EPS = 1e-6
CHUNK = 64
GLA_TAU = 16.0
GLA_BLOCK = 64
N_MIXERS = 3

V7X_VMEM_LIMIT_BYTES = 56 * 1024 * 1024
V7X_VMEM_LIMIT_RESIDENT_BYTES = 60 * 1024 * 1024
LANES = 128
FFN_PAD = 1024
PCG_ROW_CHUNK = 256
PROJ_TN = 512
OUT_TK = 512
CAST_TILE = 1024
FFN_CAST_TILE = 256


def _cparams(n_axes, vmem_limit=V7X_VMEM_LIMIT_BYTES):
    return pltpu.CompilerParams(dimension_semantics=("arbitrary",) * n_axes, vmem_limit_bytes=vmem_limit)


def _dot(a, b):
    return jnp.dot(a, b, preferred_element_type=F32)


def _rms(x):
    return x * lax.rsqrt(jnp.mean(x * x, axis=-1, keepdims=True) + EPS)


def _sigmoid(x):
    return 1.0 / (1.0 + jnp.exp(-x))


def _rms_cast_kernel(x_ref, g_ref, o_ref):
    o_ref[...] = (_rms(x_ref[...]) * g_ref[...]).astype(o_ref.dtype)


def _rms_cast(x, g, *, tm):
    m, d = x.shape
    return pl.pallas_call(
        _rms_cast_kernel,
        grid=(m // tm,),
        in_specs=[pl.BlockSpec((tm, d), lambda i: (i, 0)), pl.BlockSpec((1, d), lambda i: (0, 0))],
        out_specs=pl.BlockSpec((tm, d), lambda i: (i, 0)),
        out_shape=jax.ShapeDtypeStruct((m, d), BF16),
        compiler_params=_cparams(1),
        name="rms_cast",
    )(x, g)


def _proj_kernel(h_ref, w_ref, o_ref, *, act):
    tm = h_ref.shape[0]
    rc = min(tm, PCG_ROW_CHUNK)
    for c in range(tm // rc):
        rows = slice(c * rc, (c + 1) * rc)
        acc = _dot(h_ref[rows, :], w_ref[...])
        if act == "gelu":
            acc = 0.5 * acc * (1.0 + jnp.tanh(0.7978845608028654 * (acc + 0.044715 * (acc * acc * acc))))
        o_ref[rows, :] = acc.astype(o_ref.dtype)


def _proj(h, w, layer, *, act=None, tm, tn):
    m, k = h.shape
    n = w.shape[2]
    return pl.pallas_call(
        functools.partial(_proj_kernel, act=act),
        grid=(n // tn, m // tm),
        in_specs=[pl.BlockSpec((tm, k), lambda j, i: (i, 0)), pl.BlockSpec((None, k, tn), lambda j, i: (layer, 0, j))],
        out_specs=pl.BlockSpec((tm, tn), lambda j, i: (i, j)),
        out_shape=jax.ShapeDtypeStruct((m, n), BF16),
        compiler_params=_cparams(2),
        name="proj_" + (act or "plain"),
    )(h, w)


def _cast_kernel(src_ref, dst_ref, *, src_rows, group_cols, masked):
    v = src_ref[...]
    if masked:
        tr, tc = v.shape
        rows = lax.broadcasted_iota(jnp.int32, (tr, 1), 0) + pl.program_id(1) * tr
        cols = lax.broadcasted_iota(jnp.int32, (1, tc), 1) + pl.program_id(3) * tc
        v = jnp.where((rows < src_rows) & (cols < group_cols), v, 0.0)
    dst_ref[...] = v.astype(dst_ref.dtype)


def _cast_weights(src, *, out_rows=None, out_group_cols=None, groups=1, group_cols=None, col_offset=0, tr, tc):
    nl, src_rows, src_cols = src.shape
    out_rows = out_rows or src_rows
    group_cols = group_cols or src_cols
    out_group_cols = out_group_cols or group_cols
    assert out_rows % tr == 0 and out_group_cols % tc == 0 and col_offset % tc == 0
    assert groups == 1 or group_cols % tc == 0
    masked = out_rows != src_rows or out_group_cols != group_cols
    last_rb = (src_rows - 1) // tr
    last_cb = (src_cols - 1) // tc
    ncb = out_group_cols // tc
    src_map = lambda l, r, g, c: (l, jnp.minimum(r, last_rb),
                                  jnp.minimum((col_offset + g * group_cols) // tc + c, last_cb))
    return pl.pallas_call(
        functools.partial(_cast_kernel, src_rows=src_rows, group_cols=group_cols, masked=masked),
        grid=(nl, out_rows // tr, groups, ncb),
        in_specs=[pl.BlockSpec((None, tr, tc), src_map)],
        out_specs=pl.BlockSpec((None, tr, tc), lambda l, r, g, c: (l, r, g * ncb + c)),
        out_shape=jax.ShapeDtypeStruct((nl, out_rows, groups * out_group_cols), BF16),
        compiler_params=_cparams(4),
        name="cast_weights",
    )(src)


def _pcg_kernel(*refs, mode, tm, seq, nseg):
    if mode == "ffn":
        h_ref, wz_ref, wg_ref, cw_ref, pre_ref, o_ref, cache_ref, scr = refs
    else:
        h_ref, wg_ref, wz_ref, wx_ref, cw_ref, pre_ref, o_ref, cache_ref, scr = refs
    i = pl.program_id(1)
    cw = cw_ref[...]

    if nseg == 1:
        first = (i % (seq // tm)) == 0

        @pl.when(first)
        def _():
            scr[6:8, :] = pre_ref[0]

        @pl.when(jnp.logical_not(first))
        def _():
            scr[6:8, :] = scr[tm + 6:tm + 8, :]
    else:
        scr[6:8, :] = pre_ref[0]

    rc = min(tm, PCG_ROW_CHUNK)
    for c in range(tm // rc):
        r0 = c * rc
        hv = h_ref[r0:r0 + rc, :]
        zc = _dot(hv, wz_ref[...])
        if mode == "sc":
            zc = zc * _dot(hv, wx_ref[...])
        gate = _dot(hv, wg_ref[...])
        scr[8 + r0:8 + r0 + rc, :] = zc
        sh1 = scr[7 + r0:7 + r0 + rc, :]
        sh2 = scr[6 + r0:6 + r0 + rc, :]
        if nseg > 1:
            row = lax.broadcasted_iota(jnp.int32, (rc, 1), 0) + r0
            for s in range(nseg):
                p = pre_ref[s]
                sh1 = jnp.where(row == s * seq, p[1:2, :], sh1)
                sh2 = jnp.where(row == s * seq, p[0:1, :], sh2)
                sh2 = jnp.where(row == s * seq + 1, p[1:2, :], sh2)
        y = cw[0:1, :] * sh2 + cw[1:2, :] * sh1 + cw[2:3, :] * zc
        if mode == "ffn":
            out = (y * _sigmoid(y)) * gate
        else:
            out = gate * y
        o_ref[r0:r0 + rc, :] = out.astype(o_ref.dtype)
    if nseg == 1:
        cache_ref[0] = scr[tm + 6:tm + 8, :]
    else:
        for s in range(nseg):
            cache_ref[s] = scr[8 + (s + 1) * seq - 2:8 + (s + 1) * seq, :]


def _proj_conv_gate(h, w, layer, conv_w, prefix, *, mode, seq, tm, tn):
    m, k = h.shape
    width = conv_w.shape[1]
    nj = width // tn
    assert seq >= 2 and (seq % tm == 0 or tm % seq == 0)
    nseg = max(1, tm // seq)
    if nseg == 1:
        tiles_per_seq = seq // tm
        pre_map = lambda j, i: (i // tiles_per_seq, 0, j)
    else:
        pre_map = lambda j, i: (i, 0, j)
    n_groups = 2 if mode == "ffn" else 3
    w_specs = [pl.BlockSpec((None, k, tn), functools.partial(lambda j, i, g: (layer, 0, j + g * nj), g=g))
               for g in range(n_groups)]
    nbatch = m // seq
    out, cache = pl.pallas_call(
        functools.partial(_pcg_kernel, mode=mode, tm=tm, seq=seq, nseg=nseg),
        grid=(nj, m // tm),
        in_specs=[pl.BlockSpec((tm, k), lambda j, i: (i, 0))] + w_specs + [
            pl.BlockSpec((3, tn), lambda j, i: (0, j)),
            pl.BlockSpec((nseg, 2, tn), pre_map),
        ],
        out_specs=[pl.BlockSpec((tm, tn), lambda j, i: (i, j)), pl.BlockSpec((nseg, 2, tn), pre_map)],
        out_shape=[jax.ShapeDtypeStruct((m, width), BF16), jax.ShapeDtypeStruct((nbatch, 2, width), F32)],
        scratch_shapes=[pltpu.VMEM((tm + 8, tn), F32)],
        compiler_params=_cparams(2),
        name="proj_conv_gate_" + mode,
    )(h, *([w] * n_groups), conv_w, prefix)
    return out, cache


def _sg_mix_kernel(zu_ref, zv_ref, lng_ref, wm_ref, bias_ref, o_ref, *v_refs, heads):
    vpre = zv_ref[...].astype(F32)
    xc = vpre - jnp.mean(vpre, axis=-1, keepdims=True)
    v = xc * lax.rsqrt(jnp.mean(xc * xc, axis=-1, keepdims=True) + EPS) * lng_ref[...]
    if v_refs:
        v_refs[0][...] = v
    vb = v.astype(BF16)
    hd = vb.shape[1] // heads
    bias = bias_ref[...]
    for h in range(heads):
        mixed = _dot(wm_ref[h], vb[:, h * hd:(h + 1) * hd]) + bias[:, h:h + 1]
        o_ref[:, h * hd:(h + 1) * hd] = (zu_ref[:, h * hd:(h + 1) * hd].astype(F32) * mixed).astype(o_ref.dtype)


def _sg_mix(z, ln_g, wm, bias, *, emit_v):
    m = z.shape[0]
    width = z.shape[1] // 2
    heads, ln = wm.shape[0], wm.shape[1]
    out_shape = [jax.ShapeDtypeStruct((m, width), BF16)]
    out_specs = [pl.BlockSpec((ln, width), lambda c: (c, 0))]
    if emit_v:
        out_shape.append(jax.ShapeDtypeStruct((m, width), F32))
        out_specs.append(pl.BlockSpec((ln, width), lambda c: (c, 0)))
    res = pl.pallas_call(
        functools.partial(_sg_mix_kernel, heads=heads),
        grid=(m // ln,),
        in_specs=[
            pl.BlockSpec((ln, width), lambda c: (c, 0)),
            pl.BlockSpec((ln, width), lambda c: (c, 1)),
            pl.BlockSpec((1, width), lambda c: (0, 0)),
            pl.BlockSpec((heads, ln, ln), lambda c: (0, 0, 0)),
            pl.BlockSpec((ln, heads), lambda c: (0, 0)),
        ],
        out_specs=out_specs,
        out_shape=out_shape,
        compiler_params=_cparams(1),
        name="sg_mix",
    )(z, z, ln_g, wm, bias)
    return (res[0], res[1]) if emit_v else (res[0], None)


def _gla_kernel(q_ref, k_ref, v_ref, r_ref, glr_ref, wg_ref, bg_ref, ng_ref, s0_ref, o_ref, sout_ref, s_scr,
                *, blk, nblk, scale):
    n = pl.program_id(2)

    @pl.when(n == 0)
    def _():
        s_scr[...] = s0_ref[...]

    pre = _dot(glr_ref[...], wg_ref[...]) + bg_ref[...]
    la = (jnp.minimum(pre, 0.0) - jnp.log1p(jnp.exp(-jnp.abs(pre)))) * (1.0 / GLA_TAU)
    ri = lax.broadcasted_iota(jnp.int32, (blk, blk), 0)
    ci = lax.broadcasted_iota(jnp.int32, (blk, blk), 1)
    causal = ri >= ci
    tri = jnp.where(causal, 1.0, 0.0).astype(BF16)
    la1 = la.astype(BF16)
    rem = la - la1.astype(F32)
    la2 = rem.astype(BF16)
    la3 = (rem - la2.astype(F32)).astype(BF16)
    g = _dot(tri, la1) + _dot(tri, la2) + _dot(tri, la3)
    g_last = g[blk - 1:blk, :]
    q = q_ref[...].astype(F32) * scale
    k = k_ref[...].astype(F32)
    q_t = (q * jnp.exp(g)).astype(BF16)
    k_t = (k * jnp.exp(-g)).astype(BF16)
    k_end = (k * jnp.exp(g_last - g)).astype(BF16)
    att = lax.dot_general(q_t, k_t, (((1,), (1,)), ((), ())), preferred_element_type=F32)
    att = jnp.where(causal, att, 0.0)
    v = v_ref[...]
    s = s_scr[...]
    o = _dot(att.astype(BF16), v) + _dot(q_t, s.astype(BF16))
    decay = jnp.transpose(jnp.broadcast_to(jnp.exp(g_last), (LANES, g.shape[1])))[:, 0:1]
    s_new = decay * s + lax.dot_general(k_end, v, (((0,), (0,)), ((), ())), preferred_element_type=F32)
    s_scr[...] = s_new

    @pl.when(n == nblk - 1)
    def _():
        sout_ref[...] = s_new

    r = r_ref[...].astype(F32)
    o_ref[...] = (_rms(o) * ng_ref[...] * (r * _sigmoid(r))).astype(o_ref.dtype)


def _gla_core(proj, glr, w_gate, b_gate, norm_g, s0, *, seq, heads):
    m = proj.shape[0]
    nbatch = m // seq
    dk, dv = s0.shape[2], s0.shape[3]
    blk = GLA_BLOCK if seq % GLA_BLOCK == 0 else seq
    nblk = seq // blk
    kq = heads * dk // dk
    row = lambda b, h, n: b * nblk + n
    out, s_new = pl.pallas_call(
        functools.partial(_gla_kernel, blk=blk, nblk=nblk, scale=float(dk) ** -0.5),
        grid=(nbatch, heads, nblk),
        in_specs=[
            pl.BlockSpec((blk, dk), lambda b, h, n: (row(b, h, n), h)),
            pl.BlockSpec((blk, dk), lambda b, h, n: (row(b, h, n), kq + h)),
            pl.BlockSpec((blk, dv), lambda b, h, n: (row(b, h, n), (2 * heads * dk) // dv + h)),
            pl.BlockSpec((blk, dv), lambda b, h, n: (row(b, h, n), (2 * heads * dk) // dv + heads + h)),
            pl.BlockSpec((blk, LANES), lambda b, h, n: (row(b, h, n), 0)),
            pl.BlockSpec((LANES, dk), lambda b, h, n: (0, h)),
            pl.BlockSpec((1, dk), lambda b, h, n: (0, h)),
            pl.BlockSpec((1, dv), lambda b, h, n: (0, 0)),
            pl.BlockSpec((None, None, dk, dv), lambda b, h, n: (b, h, 0, 0)),
        ],
        out_specs=[
            pl.BlockSpec((blk, dv), lambda b, h, n: (row(b, h, n), h)),
            pl.BlockSpec((None, None, dk, dv), lambda b, h, n: (b, h, 0, 0)),
        ],
        out_shape=[jax.ShapeDtypeStruct((m, heads * dv), BF16), jax.ShapeDtypeStruct(s0.shape, F32)],
        scratch_shapes=[pltpu.VMEM((dk, dv), F32)],
        compiler_params=_cparams(3),
        name="gla_core",
    )(proj, proj, proj, proj, glr, w_gate, b_gate, norm_g, s0)
    return out, s_new


def _qattn_kernel(h_ref, wq_ref, k_ref, v_ref, o_ref, q_scr, *, ts, scale):
    s = pl.program_id(2)

    @pl.when(s == 0)
    def _():
        q_scr[...] = _dot(h_ref[...], wq_ref[...]).astype(BF16)

    qs = q_scr[pl.ds(pl.multiple_of(s * ts, ts), ts), :]
    sc = lax.dot_general(qs, k_ref[...].astype(BF16), (((1,), (1,)), ((), ())), preferred_element_type=F32) * scale
    p = jnp.exp(sc - jnp.max(sc, axis=-1, keepdims=True))
    p = p / jnp.sum(p, axis=-1, keepdims=True)
    o_ref[...] = _dot(p.astype(BF16), v_ref[...].astype(BF16)).astype(o_ref.dtype)


def _q_attn(h, wq, mk, mv, layer, *, seq, heads, tm):
    m, k = h.shape
    hd = wq.shape[2] // heads
    mem_len = mk.shape[2]
    ts = min(tm, seq)
    nb = tm // ts
    tiles_per_seq = seq // ts
    kv_map = lambda i, hh, s: (layer, (i * nb + s) // tiles_per_seq, 0, hh)
    return pl.pallas_call(
        functools.partial(_qattn_kernel, ts=ts, scale=float(hd) ** -0.5),
        grid=(m // tm, heads, nb),
        in_specs=[
            pl.BlockSpec((tm, k), lambda i, hh, s: (i, 0)),
            pl.BlockSpec((None, k, hd), lambda i, hh, s: (layer, 0, hh)),
            pl.BlockSpec((None, None, mem_len, hd), kv_map),
            pl.BlockSpec((None, None, mem_len, hd), kv_map),
        ],
        out_specs=pl.BlockSpec((ts, hd), lambda i, hh, s: (i * nb + s, hh)),
        out_shape=jax.ShapeDtypeStruct((m, heads * hd), BF16),
        scratch_shapes=[pltpu.VMEM((tm, hd), BF16)],
        compiler_params=_cparams(3),
        name="q_attn",
    )(h, wq, mk, mv)


def _mem_kv_kernel(mem_ref, g_ref, wk_ref, wv_ref, k_ref, v_ref, hn_scr):
    @pl.when(pl.program_id(2) == 0)
    def _():
        hn_scr[...] = (_rms(mem_ref[...]) * g_ref[...]).astype(BF16)

    hn = hn_scr[...]
    k_ref[...] = _dot(hn, wk_ref[...])
    v_ref[...] = _dot(hn, wv_ref[...])


def _mem_kv(mem, g, w_kv, *, tm, tn):
    m, d = mem.shape
    depth = w_kv.shape[0]
    n = w_kv.shape[2] // 2
    nj = n // tn
    return pl.pallas_call(
        _mem_kv_kernel,
        grid=(depth, m // tm, nj),
        in_specs=[
            pl.BlockSpec((tm, d), lambda l, i, j: (i, 0)),
            pl.BlockSpec((None, 1, d), lambda l, i, j: (l, 0, 0)),
            pl.BlockSpec((None, d, tn), lambda l, i, j: (l, 0, j)),
            pl.BlockSpec((None, d, tn), lambda l, i, j: (l, 0, j + nj)),
        ],
        out_specs=[pl.BlockSpec((None, tm, tn), lambda l, i, j: (l, i, j)),
                   pl.BlockSpec((None, tm, tn), lambda l, i, j: (l, i, j))],
        out_shape=[jax.ShapeDtypeStruct((depth, m, n), F32), jax.ShapeDtypeStruct((depth, m, n), F32)],
        scratch_shapes=[pltpu.VMEM((tm, d), BF16)],
        compiler_params=_cparams(3),
        name="mem_kv",
    )(mem, g, w_kv, w_kv)


OUT_COL_CHUNK = 1024
OUT_EPI_ROWS = 64


def _accumulate(lhs_ref, w_ref, acc_ref, first):
    lhs = lhs_ref[...]
    d = acc_ref.shape[1]
    tn = min(d, OUT_COL_CHUNK)
    for c in range(d // tn):
        cs = slice(c * tn, (c + 1) * tn)
        part = _dot(lhs, w_ref[:, cs])
        acc_ref[:, cs] = part if first else acc_ref[:, cs] + part


def _res_norm_tail(acc_ref, x_ref, gpost_ref, gnext_ref, xo_ref, hn_ref):
    gpost = gpost_ref[...]
    gnext = gnext_ref[...]
    rows = min(acc_ref.shape[0], OUT_EPI_ROWS)

    def body(c, carry):
        sl = pl.ds(pl.multiple_of(c * rows, rows), rows)
        xn = x_ref[sl, :] + _rms(acc_ref[sl, :]) * gpost
        if xo_ref is not None:
            xo_ref[sl, :] = xn
        hn_ref[sl, :] = (_rms(xn) * gnext).astype(hn_ref.dtype)
        return carry

    lax.fori_loop(0, acc_ref.shape[0] // rows, body, 0)


def _outproj_ktiled_kernel(lhs_ref, w_ref, x_ref, gpost_ref, gnext_ref, *rest, nk, emit_x):
    if emit_x:
        xo_ref, hn_ref = rest
        acc_ref = xo_ref
    else:
        hn_ref, acc_ref = rest
        xo_ref = None
    kk = pl.program_id(1)
    pl.when(kk == 0)(functools.partial(_accumulate, lhs_ref, w_ref, acc_ref, True))
    pl.when(kk > 0)(functools.partial(_accumulate, lhs_ref, w_ref, acc_ref, False))
    pl.when(kk == nk - 1)(functools.partial(_res_norm_tail, acc_ref, x_ref, gpost_ref, gnext_ref, xo_ref, hn_ref))


def _outproj_resident_kernel(lhs_ref, w_ref, x_ref, gpost_ref, gnext_ref, xo_ref, hn_ref):
    _accumulate(lhs_ref, w_ref, xo_ref, True)
    _res_norm_tail(xo_ref, x_ref, gpost_ref, gnext_ref, xo_ref, hn_ref)


def _outproj(lhs, w, layer, x, g_post, g_next, *, hn_dtype, emit_x, tm, tk):
    m, kdim = lhs.shape
    d = w.shape[2]
    out_shape = [jax.ShapeDtypeStruct((m, d), hn_dtype)]
    if emit_x:
        out_shape.insert(0, jax.ShapeDtypeStruct((m, d), F32))
    if tk is None:
        assert emit_x
        row_map = lambda i: (i, 0)
        fix_map = lambda i: (0, 0)
        return pl.pallas_call(
            _outproj_resident_kernel,
            grid=(m // tm,),
            in_specs=[
                pl.BlockSpec((tm, kdim), row_map),
                pl.BlockSpec((None, kdim, d), lambda i: (layer, 0, 0), pipeline_mode=pl.Buffered(1)),
                pl.BlockSpec((tm, d), row_map),
                pl.BlockSpec((1, d), fix_map),
                pl.BlockSpec((1, d), fix_map),
            ],
            out_specs=[pl.BlockSpec((tm, d), row_map), pl.BlockSpec((tm, d), row_map)],
            out_shape=out_shape,
            compiler_params=_cparams(1, V7X_VMEM_LIMIT_RESIDENT_BYTES),
            name="outproj_resident",
        )(lhs, w, x, g_post, g_next)
    nk = kdim // tk
    row_map = lambda i, kk: (i, 0)
    fix_map = lambda i, kk: (0, 0)
    return pl.pallas_call(
        functools.partial(_outproj_ktiled_kernel, nk=nk, emit_x=emit_x),
        grid=(m // tm, nk),
        in_specs=[
            pl.BlockSpec((tm, tk), lambda i, kk: (i, kk)),
            pl.BlockSpec((None, tk, d), lambda i, kk: (layer, kk, 0)),
            pl.BlockSpec((tm, d), row_map),
            pl.BlockSpec((1, d), fix_map),
            pl.BlockSpec((1, d), fix_map),
        ],
        out_specs=[pl.BlockSpec((tm, d), row_map)] * len(out_shape),
        out_shape=out_shape,
        scratch_shapes=[] if emit_x else [pltpu.VMEM((tm, d), F32)],
        compiler_params=_cparams(2),
        name="outproj_ktiled",
    )(lhs, w, x, g_post, g_next)


def _pad_last(a, n):
    return jnp.pad(a, [(0, 0)] * (a.ndim - 1) + [(0, n - a.shape[-1])])


def _tiles(m):
    big = m >= 1024
    return dict(
        norm=256 if big else m,
        proj=1024 if big else m,
        sc=512 if big else m,
        attn=512 if big else m,
        out=512 if big else m,
        out_resident=256 if big else m,
    )


def _sg_weights(w_s, b_s, seq):
    heads, ln, _ = w_s.shape
    pos = jnp.arange(ln) // CHUNK
    w = jnp.where(pos[None, :] <= pos[:, None], w_s, jnp.zeros_like(w_s))
    if seq % ln == 0:
        return w.astype(BF16), b_s.T
    assert ln % seq == 0
    reps = ln // seq
    blockdiag = jnp.kron(jnp.eye(reps, dtype=w.dtype), jnp.ones((seq, seq), w.dtype))
    wm = jnp.tile(w[:, :seq, :seq], (1, reps, reps)) * blockdiag
    return wm.astype(BF16), jnp.tile(b_s[:, :seq].T, (reps, 1))


def kernel(x_prompt, x_sample, mem_prompt, state_gla, cache_conv_c, cache_ffn_conv, cache_mem_k, cache_mem_v,
           norm_g, mem_norm_g, final_g,
           sg_w_in, sg_ln_g, sg_w_s, sg_b_s, sg_w_out,
           gla_w_in, gla_w_gate, gla_b_gate, gla_norm_g, gla_w_out,
           sc_w_in, sc_conv, sc_w_out,
           mem_w_q, mem_w_kv, mem_w_o,
           ffn_w_up, ffn_conv, ffn_w_down):
    depth = norm_g.shape[0]
    d_model = x_prompt.shape[-1]
    d_ff = ffn_conv.shape[-1]
    d_ffp = -(-d_ff // FFN_PAD) * FFN_PAD
    gla_heads, gla_dk, gla_dv = state_gla.shape[2], state_gla.shape[3], state_gla.shape[4]
    gla_main = 2 * gla_heads * (gla_dk + gla_dv)
    mem_heads = cache_mem_k.shape[3]
    mem_len = cache_mem_k.shape[2]
    row = lambda a: a.reshape(1, -1)

    def cast(w, **kw):
        kw.setdefault("tr", min(w.shape[1], CAST_TILE))
        kw.setdefault("tc", min(w.shape[2], CAST_TILE))
        return _cast_weights(w, **kw)

    w_sg_in, w_sg_out = cast(sg_w_in), cast(sg_w_out)
    w_gla_in = cast(gla_w_in, group_cols=gla_main)
    w_gla_lr = cast(gla_w_in, col_offset=gla_main, group_cols=gla_w_in.shape[2] - gla_main, out_group_cols=LANES,
                    tr=d_model, tc=LANES)
    w_gla_gate = [jnp.pad(gla_w_gate[b].astype(BF16), ((0, LANES - gla_w_gate.shape[1]), (0, 0)))
                  for b in range(gla_w_gate.shape[0])]
    w_gla_out = cast(gla_w_out)
    w_sc_in, w_sc_out = cast(sc_w_in), cast(sc_w_out)
    w_q, w_kv, w_o = cast(mem_w_q), cast(mem_w_kv), cast(mem_w_o)
    w_up = cast(ffn_w_up, groups=2, group_cols=d_ff, out_group_cols=d_ffp, tr=d_model, tc=FFN_CAST_TILE)
    w_down = cast(ffn_w_down, out_rows=d_ffp, tr=FFN_CAST_TILE, tc=d_model)
    ffn_cw = [_pad_last(ffn_conv[l], d_ffp) for l in range(depth)]

    def run(x, seq, mk, mv, s_gla, buf_c, buf_f, emit_v):
        m = x.shape[0]
        t = _tiles(m)
        sg_v, gla_out, c_out, f_out = [], [], [], []
        ia = ib = ic = 0
        hn = _rms_cast(x, row(norm_g[0, 0]), tm=t["norm"])
        for l in range(depth):
            kind = l % N_MIXERS
            if kind == 0:
                z = _proj(hn, w_sg_in, ia, act="gelu", tm=t["proj"], tn=PROJ_TN)
                wm, bias = _sg_weights(sg_w_s[ia], sg_b_s[ia], seq)
                lhs, v_rows = _sg_mix(z, row(sg_ln_g[ia]), wm, bias, emit_v=emit_v)
                sg_v.append(v_rows)
                w_out, i_out = w_sg_out, ia
                ia += 1
            elif kind == 1:
                proj = _proj(hn, w_gla_in, ib, tm=t["proj"], tn=PROJ_TN)
                glr = _proj(hn, w_gla_lr, ib, tm=t["proj"], tn=LANES)
                lhs, s_new = _gla_core(proj, glr, w_gla_gate[ib], row(gla_b_gate[ib]), row(gla_norm_g[ib]),
                                       s_gla[ib], seq=seq, heads=gla_heads)
                gla_out.append(s_new)
                w_out, i_out = w_gla_out, ib
                ib += 1
            else:
                lhs, cb = _proj_conv_gate(hn, w_sc_in, ic, sc_conv[ic], buf_c[ic], mode="sc", seq=seq,
                                          tm=t["sc"], tn=PROJ_TN)
                c_out.append(cb)
                w_out, i_out = w_sc_out, ic
                ic += 1
            x, hn = _outproj(lhs, w_out, i_out, x, row(norm_g[l, 1]), row(norm_g[l, 2]), hn_dtype=BF16, emit_x=True,
                             tm=t["out_resident"], tk=None)
            att = _q_attn(hn, w_q, mk, mv, l, seq=seq, heads=mem_heads, tm=t["attn"])
            x, hn = _outproj(att, w_o, l, x, row(norm_g[l, 3]), row(norm_g[l, 4]), hn_dtype=BF16, emit_x=True,
                             tm=t["out_resident"], tk=None)
            hid, fb = _proj_conv_gate(hn, w_up, l, ffn_cw[l], _pad_last(buf_f[l], d_ffp), mode="ffn", seq=seq,
                                      tm=t["proj"], tn=PROJ_TN)
            f_out.append(fb[:, :, :d_ff])
            if l < depth - 1:
                x, hn = _outproj(hid, w_down, l, x, row(norm_g[l, 5]), row(norm_g[l + 1, 0]), hn_dtype=BF16,
                                 emit_x=True, tm=t["out"], tk=OUT_TK)
            else:
                hn, = _outproj(hid, w_down, l, x, row(norm_g[l, 5]), row(final_g), hn_dtype=F32,
                               emit_x=False, tm=t["out"], tk=OUT_TK)
        return hn, sg_v, gla_out, c_out, f_out

    bp, sp, _ = x_prompt.shape
    bs, ss, _ = x_sample.shape

    mem2d = mem_prompt.reshape(bp * mem_len, d_model)
    mem_k_p, mem_v_p = _mem_kv(mem2d, mem_norm_g.reshape(depth, 1, d_model), w_kv, tm=512, tn=512)
    mem_k_p = mem_k_p.reshape(depth, bp, mem_len, d_model)
    mem_v_p = mem_v_p.reshape(depth, bp, mem_len, d_model)
    y_p, _, gla_p, conv_p, ffn_p = run(
        x_prompt.reshape(bp * sp, d_model), sp, mem_k_p, mem_v_p,
        jnp.zeros((state_gla.shape[0], bp) + state_gla.shape[2:], F32),
        jnp.zeros((cache_conv_c.shape[0], bp, 2, d_model), F32),
        jnp.zeros((depth, bp, 2, d_ff), F32), emit_v=False)

    y_s, sgv_s, gla_s, conv_s, ffn_s = run(
        x_sample.reshape(bs * ss, d_model), ss,
        cache_mem_k.reshape(depth, bs, mem_len, -1), cache_mem_v.reshape(depth, bs, mem_len, -1),
        state_gla, cache_conv_c, cache_ffn_conv, emit_v=True)

    kv_shape = (depth, bp, mem_len, mem_heads, d_model // mem_heads)
    return (y_p.reshape(bp, sp, d_model), y_s.reshape(bs, ss, d_model),
            jnp.stack(gla_p), jnp.stack(conv_p), jnp.stack(ffn_p),
            mem_k_p.reshape(kv_shape), mem_v_p.reshape(kv_shape),
            jnp.stack([v.reshape(bs, ss, -1) for v in sgv_s]), jnp.stack(gla_s), jnp.stack(conv_s), jnp.stack(ffn_s))
```

```python
import functools

import jax
import jax.numpy as jnp
from jax import lax
from jax.experimental import pallas as pl
from jax.experimental.pallas import tpu as pltpu

F32 = jnp.float32
BF16 = jnp.bfloat16

EPS = 1e-6
CHUNK = 64
GLA_TAU = 16.0
GLA_BLOCK = 64
N_MIXERS = 3

V7X_VMEM_LIMIT_BYTES = 56 * 1024 * 1024
V7X_VMEM_LIMIT_RESIDENT_BYTES = 60 * 1024 * 1024
LANES = 128
FFN_PAD = 1024
PCG_ROW_CHUNK = 256
PROJ_TN = 512
OUT_TK = 1024
CAST_TILE = 1024
FFN_CAST_TILE = 256


def _cparams(n_axes, vmem_limit=V7X_VMEM_LIMIT_BYTES):
    return pltpu.CompilerParams(dimension_semantics=("arbitrary",) * n_axes, vmem_limit_bytes=vmem_limit)


def _dot(a, b):
    return jnp.dot(a, b, preferred_element_type=F32)


def _rms(x):
    return x * lax.rsqrt(jnp.mean(x * x, axis=-1, keepdims=True) + EPS)


def _sigmoid(x):
    return 1.0 / (1.0 + jnp.exp(-x))


def _rms_cast_kernel(x_ref, g_ref, o_ref):
    o_ref[...] = (_rms(x_ref[...]) * g_ref[...]).astype(o_ref.dtype)


def _rms_cast(x, g, *, tm):
    m, d = x.shape
    return pl.pallas_call(
        _rms_cast_kernel,
        grid=(m // tm,),
        in_specs=[pl.BlockSpec((tm, d), lambda i: (i, 0)), pl.BlockSpec((1, d), lambda i: (0, 0))],
        out_specs=pl.BlockSpec((tm, d), lambda i: (i, 0)),
        out_shape=jax.ShapeDtypeStruct((m, d), BF16),
        compiler_params=_cparams(1),
        name="rms_cast",
    )(x, g)


def _proj_kernel(h_ref, w_ref, o_ref, *, act):
    tm = h_ref.shape[0]
    rc = min(tm, PCG_ROW_CHUNK)
    for c in range(tm // rc):
        rows = slice(c * rc, (c + 1) * rc)
        acc = _dot(h_ref[rows, :], w_ref[...])
        if act == "gelu":
            acc = 0.5 * acc * (1.0 + jnp.tanh(0.7978845608028654 * (acc + 0.044715 * (acc * acc * acc))))
        o_ref[rows, :] = acc.astype(o_ref.dtype)


def _proj(h, w, layer, *, act=None, tm, tn):
    m, k = h.shape
    n = w.shape[2]
    return pl.pallas_call(
        functools.partial(_proj_kernel, act=act),
        grid=(n // tn, m // tm),
        in_specs=[pl.BlockSpec((tm, k), lambda j, i: (i, 0)), pl.BlockSpec((None, k, tn), lambda j, i: (layer, 0, j))],
        out_specs=pl.BlockSpec((tm, tn), lambda j, i: (i, j)),
        out_shape=jax.ShapeDtypeStruct((m, n), BF16),
        compiler_params=_cparams(2),
        name="proj_" + (act or "plain"),
    )(h, w)


def _cast_kernel(src_ref, dst_ref, *, src_rows, group_cols, masked):
    v = src_ref[...]
    if masked:
        tr, tc = v.shape
        rows = lax.broadcasted_iota(jnp.int32, (tr, 1), 0) + pl.program_id(1) * tr
        cols = lax.broadcasted_iota(jnp.int32, (1, tc), 1) + pl.program_id(3) * tc
        v = jnp.where((rows < src_rows) & (cols < group_cols), v, 0.0)
    dst_ref[...] = v.astype(dst_ref.dtype)


def _cast_weights(src, *, out_rows=None, out_group_cols=None, groups=1, group_cols=None, col_offset=0, tr, tc):
    nl, src_rows, src_cols = src.shape
    out_rows = out_rows or src_rows
    group_cols = group_cols or src_cols
    out_group_cols = out_group_cols or group_cols
    assert out_rows % tr == 0 and out_group_cols % tc == 0 and col_offset % tc == 0
    assert groups == 1 or group_cols % tc == 0
    masked = out_rows != src_rows or out_group_cols != group_cols
    last_rb = (src_rows - 1) // tr
    last_cb = (src_cols - 1) // tc
    ncb = out_group_cols // tc
    src_map = lambda l, r, g, c: (l, jnp.minimum(r, last_rb),
                                  jnp.minimum((col_offset + g * group_cols) // tc + c, last_cb))
    return pl.pallas_call(
        functools.partial(_cast_kernel, src_rows=src_rows, group_cols=group_cols, masked=masked),
        grid=(nl, out_rows // tr, groups, ncb),
        in_specs=[pl.BlockSpec((None, tr, tc), src_map)],
        out_specs=pl.BlockSpec((None, tr, tc), lambda l, r, g, c: (l, r, g * ncb + c)),
        out_shape=jax.ShapeDtypeStruct((nl, out_rows, groups * out_group_cols), BF16),
        compiler_params=_cparams(4),
        name="cast_weights",
    )(src)


def _pcg_kernel(*refs, mode, tm, seq, nseg):
    if mode == "ffn":
        h_ref, wz_ref, wg_ref, cw_ref, pre_ref, o_ref, cache_ref, scr = refs
    else:
        h_ref, wg_ref, wz_ref, wx_ref, cw_ref, pre_ref, o_ref, cache_ref, scr = refs
    i = pl.program_id(1)
    cw = cw_ref[...]

    if nseg == 1:
        first = (i % (seq // tm)) == 0

        @pl.when(first)
        def _():
            scr[6:8, :] = pre_ref[0]

        @pl.when(jnp.logical_not(first))
        def _():
            scr[6:8, :] = scr[tm + 6:tm + 8, :]
    else:
        scr[6:8, :] = pre_ref[0]

    rc = min(tm, PCG_ROW_CHUNK)
    for c in range(tm // rc):
        r0 = c * rc
        hv = h_ref[r0:r0 + rc, :]
        zc = _dot(hv, wz_ref[...])
        if mode == "sc":
            zc = zc * _dot(hv, wx_ref[...])
        gate = _dot(hv, wg_ref[...])
        scr[8 + r0:8 + r0 + rc, :] = zc
        sh1 = scr[7 + r0:7 + r0 + rc, :]
        sh2 = scr[6 + r0:6 + r0 + rc, :]
        if nseg > 1:
            row = lax.broadcasted_iota(jnp.int32, (rc, 1), 0) + r0
            for s in range(nseg):
                p = pre_ref[s]
                sh1 = jnp.where(row == s * seq, p[1:2, :], sh1)
                sh2 = jnp.where(row == s * seq, p[0:1, :], sh2)
                sh2 = jnp.where(row == s * seq + 1, p[1:2, :], sh2)
        y = cw[0:1, :] * sh2 + cw[1:2, :] * sh1 + cw[2:3, :] * zc
        if mode == "ffn":
            out = (y * _sigmoid(y)) * gate
        else:
            out = gate * y
        o_ref[r0:r0 + rc, :] = out.astype(o_ref.dtype)
    if nseg == 1:
        cache_ref[0] = scr[tm + 6:tm + 8, :]
    else:
        for s in range(nseg):
            cache_ref[s] = scr[8 + (s + 1) * seq - 2:8 + (s + 1) * seq, :]


def _proj_conv_gate(h, w, layer, conv_w, prefix, *, mode, seq, tm, tn):
    m, k = h.shape
    width = conv_w.shape[1]
    nj = width // tn
    assert seq >= 2 and (seq % tm == 0 or tm % seq == 0)
    nseg = max(1, tm // seq)
    if nseg == 1:
        tiles_per_seq = seq // tm
        pre_map = lambda j, i: (i // tiles_per_seq, 0, j)
    else:
        pre_map = lambda j, i: (i, 0, j)
    n_groups = 2 if mode == "ffn" else 3
    w_specs = [pl.BlockSpec((None, k, tn), functools.partial(lambda j, i, g: (layer, 0, j + g * nj), g=g))
               for g in range(n_groups)]
    nbatch = m // seq
    out, cache = pl.pallas_call(
        functools.partial(_pcg_kernel, mode=mode, tm=tm, seq=seq, nseg=nseg),
        grid=(nj, m // tm),
        in_specs=[pl.BlockSpec((tm, k), lambda j, i: (i, 0))] + w_specs + [
            pl.BlockSpec((3, tn), lambda j, i: (0, j)),
            pl.BlockSpec((nseg, 2, tn), pre_map),
        ],
        out_specs=[pl.BlockSpec((tm, tn), lambda j, i: (i, j)), pl.BlockSpec((nseg, 2, tn), pre_map)],
        out_shape=[jax.ShapeDtypeStruct((m, width), BF16), jax.ShapeDtypeStruct((nbatch, 2, width), F32)],
        scratch_shapes=[pltpu.VMEM((tm + 8, tn), F32)],
        compiler_params=_cparams(2),
        name="proj_conv_gate_" + mode,
    )(h, *([w] * n_groups), conv_w, prefix)
    return out, cache


def _sg_mix_kernel(zu_ref, zv_ref, lng_ref, wm_ref, bias_ref, o_ref, *v_refs, heads):
    vpre = zv_ref[...].astype(F32)
    xc = vpre - jnp.mean(vpre, axis=-1, keepdims=True)
    v = xc * lax.rsqrt(jnp.mean(xc * xc, axis=-1, keepdims=True) + EPS) * lng_ref[...]
    if v_refs:
        v_refs[0][...] = v
    vb = v.astype(BF16)
    hd = vb.shape[1] // heads
    bias = bias_ref[...]
    for h in range(heads):
        mixed = _dot(wm_ref[h], vb[:, h * hd:(h + 1) * hd]) + bias[:, h:h + 1]
        o_ref[:, h * hd:(h + 1) * hd] = (zu_ref[:, h * hd:(h + 1) * hd].astype(F32) * mixed).astype(o_ref.dtype)


def _sg_mix(z, ln_g, wm, bias, *, emit_v):
    m = z.shape[0]
    width = z.shape[1] // 2
    heads, ln = wm.shape[0], wm.shape[1]
    out_shape = [jax.ShapeDtypeStruct((m, width), BF16)]
    out_specs = [pl.BlockSpec((ln, width), lambda c: (c, 0))]
    if emit_v:
        out_shape.append(jax.ShapeDtypeStruct((m, width), F32))
        out_specs.append(pl.BlockSpec((ln, width), lambda c: (c, 0)))
    res = pl.pallas_call(
        functools.partial(_sg_mix_kernel, heads=heads),
        grid=(m // ln,),
        in_specs=[
            pl.BlockSpec((ln, width), lambda c: (c, 0)),
            pl.BlockSpec((ln, width), lambda c: (c, 1)),
            pl.BlockSpec((1, width), lambda c: (0, 0)),
            pl.BlockSpec((heads, ln, ln), lambda c: (0, 0, 0)),
            pl.BlockSpec((ln, heads), lambda c: (0, 0)),
        ],
        out_specs=out_specs,
        out_shape=out_shape,
        compiler_params=_cparams(1),
        name="sg_mix",
    )(z, z, ln_g, wm, bias)
    return (res[0], res[1]) if emit_v else (res[0], None)


def _gla_kernel(q_ref, k_ref, v_ref, r_ref, glr_ref, wg_ref, bg_ref, ng_ref, s0_ref, o_ref, sout_ref, s_scr,
                *, blk, nblk, scale):
    n = pl.program_id(2)

    @pl.when(n == 0)
    def _():
        s_scr[...] = s0_ref[...]

    pre = _dot(glr_ref[...], wg_ref[...]) + bg_ref[...]
    la = (jnp.minimum(pre, 0.0) - jnp.log1p(jnp.exp(-jnp.abs(pre)))) * (1.0 / GLA_TAU)
    ri = lax.broadcasted_iota(jnp.int32, (blk, blk), 0)
    ci = lax.broadcasted_iota(jnp.int32, (blk, blk), 1)
    causal = ri >= ci
    tri = jnp.where(causal, 1.0, 0.0).astype(BF16)
    la1 = la.astype(BF16)
    rem = la - la1.astype(F32)
    la2 = rem.astype(BF16)
    la3 = (rem - la2.astype(F32)).astype(BF16)
    g = _dot(tri, la1) + _dot(tri, la2) + _dot(tri, la3)
    g_last = g[blk - 1:blk, :]
    q = q_ref[...].astype(F32) * scale
    k = k_ref[...].astype(F32)
    q_t = (q * jnp.exp(g)).astype(BF16)
    k_t = (k * jnp.exp(-g)).astype(BF16)
    k_end = (k * jnp.exp(g_last - g)).astype(BF16)
    att = lax.dot_general(q_t, k_t, (((1,), (1,)), ((), ())), preferred_element_type=F32)
    att = jnp.where(causal, att, 0.0)
    v = v_ref[...]
    s = s_scr[...]
    o = _dot(att.astype(BF16), v) + _dot(q_t, s.astype(BF16))
    decay = jnp.transpose(jnp.broadcast_to(jnp.exp(g_last), (LANES, g.shape[1])))[:, 0:1]
    s_new = decay * s + lax.dot_general(k_end, v, (((0,), (0,)), ((), ())), preferred_element_type=F32)
    s_scr[...] = s_new

    @pl.when(n == nblk - 1)
    def _():
        sout_ref[...] = s_new

    r = r_ref[...].astype(F32)
    o_ref[...] = (_rms(o) * ng_ref[...] * (r * _sigmoid(r))).astype(o_ref.dtype)


def _gla_core(proj, glr, w_gate, b_gate, norm_g, s0, *, seq, heads):
    m = proj.shape[0]
    nbatch = m // seq
    dk, dv = s0.shape[2], s0.shape[3]
    blk = GLA_BLOCK if seq % GLA_BLOCK == 0 else seq
    nblk = seq // blk
    kq = heads * dk // dk
    row = lambda b, h, n: b * nblk + n
    out, s_new = pl.pallas_call(
        functools.partial(_gla_kernel, blk=blk, nblk=nblk, scale=float(dk) ** -0.5),
        grid=(nbatch, heads, nblk),
        in_specs=[
            pl.BlockSpec((blk, dk), lambda b, h, n: (row(b, h, n), h)),
            pl.BlockSpec((blk, dk), lambda b, h, n: (row(b, h, n), kq + h)),
            pl.BlockSpec((blk, dv), lambda b, h, n: (row(b, h, n), (2 * heads * dk) // dv + h)),
            pl.BlockSpec((blk, dv), lambda b, h, n: (row(b, h, n), (2 * heads * dk) // dv + heads + h)),
            pl.BlockSpec((blk, LANES), lambda b, h, n: (row(b, h, n), 0)),
            pl.BlockSpec((LANES, dk), lambda b, h, n: (0, h)),
            pl.BlockSpec((1, dk), lambda b, h, n: (0, h)),
            pl.BlockSpec((1, dv), lambda b, h, n: (0, 0)),
            pl.BlockSpec((None, None, dk, dv), lambda b, h, n: (b, h, 0, 0)),
        ],
        out_specs=[
            pl.BlockSpec((blk, dv), lambda b, h, n: (row(b, h, n), h)),
            pl.BlockSpec((None, None, dk, dv), lambda b, h, n: (b, h, 0, 0)),
        ],
        out_shape=[jax.ShapeDtypeStruct((m, heads * dv), BF16), jax.ShapeDtypeStruct(s0.shape, F32)],
        scratch_shapes=[pltpu.VMEM((dk, dv), F32)],
        compiler_params=_cparams(3),
        name="gla_core",
    )(proj, proj, proj, proj, glr, w_gate, b_gate, norm_g, s0)
    return out, s_new


def _qattn_kernel(h_ref, wq_ref, k_ref, v_ref, o_ref, *scratch, ts, nb, scale):
    kb = k_ref[...].astype(BF16)
    vb = v_ref[...].astype(BF16)

    def attend(q):
        sc = lax.dot_general(q, kb, (((1,), (1,)), ((), ())), preferred_element_type=F32) * scale
        p = jnp.exp(sc - jnp.max(sc, axis=-1, keepdims=True))
        p = p / jnp.sum(p, axis=-1, keepdims=True)
        return _dot(p.astype(BF16), vb).astype(o_ref.dtype)

    if nb == 1:
        o_ref[...] = attend(_dot(h_ref[...], wq_ref[...]).astype(BF16))
    else:
        q_scr, = scratch
        s = pl.program_id(2)

        @pl.when(s == 0)
        def _():
            q_scr[...] = _dot(h_ref[...], wq_ref[...]).astype(BF16)

        o_ref[...] = attend(q_scr[pl.ds(pl.multiple_of(s * ts, ts), ts), :])


def _q_attn(h, wq, mk, mv, layer, *, seq, heads, tm):
    m, k = h.shape
    hd = wq.shape[2] // heads
    mem_len = mk.shape[2]
    ts = min(tm, seq)
    nb = tm // ts
    tiles_per_seq = seq // ts
    kv_map = lambda i, hh, s: (layer, (i * nb + s) // tiles_per_seq, 0, hh)
    return pl.pallas_call(
        functools.partial(_qattn_kernel, ts=ts, nb=nb, scale=float(hd) ** -0.5),
        grid=(m // tm, heads, nb),
        in_specs=[
            pl.BlockSpec((tm, k), lambda i, hh, s: (i, 0)),
            pl.BlockSpec((None, k, hd), lambda i, hh, s: (layer, 0, hh)),
            pl.BlockSpec((None, None, mem_len, hd), kv_map),
            pl.BlockSpec((None, None, mem_len, hd), kv_map),
        ],
        out_specs=pl.BlockSpec((ts, hd), lambda i, hh, s: (i * nb + s, hh)),
        out_shape=jax.ShapeDtypeStruct((m, heads * hd), BF16),
        scratch_shapes=[pltpu.VMEM((tm, hd), BF16)] if nb > 1 else [],
        compiler_params=_cparams(3),
        name="q_attn",
    )(h, wq, mk, mv)


def _mem_kv_kernel(mem_ref, g_ref, wk_ref, wv_ref, k_ref, v_ref, hn_scr):
    @pl.when(pl.program_id(2) == 0)
    def _():
        hn_scr[...] = (_rms(mem_ref[...]) * g_ref[...]).astype(BF16)

    hn = hn_scr[...]
    k_ref[...] = _dot(hn, wk_ref[...])
    v_ref[...] = _dot(hn, wv_ref[...])


def _mem_kv(mem, g, w_kv, *, tm, tn):
    m, d = mem.shape
    depth = w_kv.shape[0]
    n = w_kv.shape[2] // 2
    nj = n // tn
    return pl.pallas_call(
        _mem_kv_kernel,
        grid=(depth, m // tm, nj),
        in_specs=[
            pl.BlockSpec((tm, d), lambda l, i, j: (i, 0)),
            pl.BlockSpec((None, 1, d), lambda l, i, j: (l, 0, 0)),
            pl.BlockSpec((None, d, tn), lambda l, i, j: (l, 0, j)),
            pl.BlockSpec((None, d, tn), lambda l, i, j: (l, 0, j + nj)),
        ],
        out_specs=[pl.BlockSpec((None, tm, tn), lambda l, i, j: (l, i, j)),
                   pl.BlockSpec((None, tm, tn), lambda l, i, j: (l, i, j))],
        out_shape=[jax.ShapeDtypeStruct((depth, m, n), F32), jax.ShapeDtypeStruct((depth, m, n), F32)],
        scratch_shapes=[pltpu.VMEM((tm, d), BF16)],
        compiler_params=_cparams(3),
        name="mem_kv",
    )(mem, g, w_kv, w_kv)


OUT_COL_CHUNK = 1024
OUT_EPI_ROWS = 64
OUT_ROW_GROUP = 128


def _accumulate(lhs_ref, w_ref, acc_ref, first):
    lhs = lhs_ref[...]
    d = acc_ref.shape[1]
    tn = min(d, OUT_COL_CHUNK)
    for c in range(d // tn):
        cs = slice(c * tn, (c + 1) * tn)
        part = _dot(lhs, w_ref[:, cs])
        acc_ref[:, cs] = part if first else acc_ref[:, cs] + part


def _res_norm_tail(acc_ref, x_ref, gpost_ref, gnext_ref, xo_ref, hn_ref, *, unrolled=False):
    gpost = gpost_ref[...]
    gnext = gnext_ref[...]
    rows = min(acc_ref.shape[0], OUT_EPI_ROWS)

    def chunk(sl):
        xn = x_ref[sl, :] + _rms(acc_ref[sl, :]) * gpost
        if xo_ref is not None:
            xo_ref[sl, :] = xn
        hn_ref[sl, :] = (_rms(xn) * gnext).astype(hn_ref.dtype)

    if unrolled:
        for c in range(acc_ref.shape[0] // rows):
            chunk(slice(c * rows, (c + 1) * rows))
    else:
        def body(c, carry):
            chunk(pl.ds(pl.multiple_of(c * rows, rows), rows))
            return carry

        lax.fori_loop(0, acc_ref.shape[0] // rows, body, 0)


def _outproj_ktiled_kernel(lhs_ref, w_ref, x_hbm, gpost_ref, gnext_ref, *rest, nk, emit_x):
    if emit_x:
        xo_ref, hn_ref, x_buf, x_sem = rest
        acc_ref = xo_ref
    else:
        hn_ref, acc_ref, x_buf, x_sem = rest
        xo_ref = None
    tm = x_buf.shape[0]
    row0 = pl.multiple_of(pl.program_id(0) * tm, tm)
    x_copy = pltpu.make_async_copy(x_hbm.at[pl.ds(row0, tm), :], x_buf, x_sem)
    kk = pl.program_id(1)

    @pl.when(kk == 0)
    def _():
        x_copy.start()
        _accumulate(lhs_ref, w_ref, acc_ref, True)

    pl.when(kk > 0)(functools.partial(_accumulate, lhs_ref, w_ref, acc_ref, False))

    @pl.when(kk == nk - 1)
    def _():
        x_copy.wait()
        _res_norm_tail(acc_ref, x_buf, gpost_ref, gnext_ref, xo_ref, hn_ref)


def _outproj_resident_kernel(lhs_ref, w_ref, x_ref, gpost_ref, gnext_ref, xo_ref, hn_ref):
    tm = xo_ref.shape[0]
    group = min(tm, OUT_ROW_GROUP)
    for r in range(tm // group):
        rows = slice(r * group, (r + 1) * group)
        _accumulate(lhs_ref.at[rows], w_ref, xo_ref.at[rows], True)
        _res_norm_tail(xo_ref.at[rows], x_ref.at[rows], gpost_ref, gnext_ref, xo_ref.at[rows], hn_ref.at[rows],
                       unrolled=True)


def _outproj(lhs, w, layer, x, g_post, g_next, *, hn_dtype, emit_x, tm, tk):
    m, kdim = lhs.shape
    d = w.shape[2]
    out_shape = [jax.ShapeDtypeStruct((m, d), hn_dtype)]
    if emit_x:
        out_shape.insert(0, jax.ShapeDtypeStruct((m, d), F32))
    if tk is None:
        assert emit_x
        row_map = lambda i: (i, 0)
        fix_map = lambda i: (0, 0)
        return pl.pallas_call(
            _outproj_resident_kernel,
            grid=(m // tm,),
            in_specs=[
                pl.BlockSpec((tm, kdim), row_map),
                pl.BlockSpec((None, kdim, d), lambda i: (layer, 0, 0), pipeline_mode=pl.Buffered(1)),
                pl.BlockSpec((tm, d), row_map),
                pl.BlockSpec((1, d), fix_map),
                pl.BlockSpec((1, d), fix_map),
            ],
            out_specs=[pl.BlockSpec((tm, d), row_map), pl.BlockSpec((tm, d), row_map)],
            out_shape=out_shape,
            compiler_params=_cparams(1, V7X_VMEM_LIMIT_RESIDENT_BYTES),
            name="outproj_resident",
        )(lhs, w, x, g_post, g_next)
    nk = kdim // tk
    row_map = lambda i, kk: (i, 0)
    fix_map = lambda i, kk: (0, 0)
    return pl.pallas_call(
        functools.partial(_outproj_ktiled_kernel, nk=nk, emit_x=emit_x),
        grid=(m // tm, nk),
        in_specs=[
            pl.BlockSpec((tm, tk), lambda i, kk: (i, kk)),
            pl.BlockSpec((None, tk, d), lambda i, kk: (layer, kk, 0)),
            pl.BlockSpec(memory_space=pl.ANY),
            pl.BlockSpec((1, d), fix_map),
            pl.BlockSpec((1, d), fix_map),
        ],
        out_specs=[pl.BlockSpec((tm, d), row_map)] * len(out_shape),
        out_shape=out_shape,
        scratch_shapes=([] if emit_x else [pltpu.VMEM((tm, d), F32)])
        + [pltpu.VMEM((tm, d), F32), pltpu.SemaphoreType.DMA(())],
        compiler_params=_cparams(2),
        name="outproj_ktiled",
    )(lhs, w, x, g_post, g_next)


def _pad_last(a, n):
    return jnp.pad(a, [(0, 0)] * (a.ndim - 1) + [(0, n - a.shape[-1])])


def _tiles(m):
    big = m >= 1024
    return dict(
        norm=256 if big else m,
        proj=1024 if big else m,
        sc=512 if big else m,
        attn=512 if big else m,
        out=512 if big else m,
        out_resident=256 if big else m,
    )


def _sg_weights(w_s, b_s, seq):
    heads, ln, _ = w_s.shape
    pos = jnp.arange(ln) // CHUNK
    w = jnp.where(pos[None, :] <= pos[:, None], w_s, jnp.zeros_like(w_s))
    if seq % ln == 0:
        return w.astype(BF16), b_s.T
    assert ln % seq == 0
    reps = ln // seq
    blockdiag = jnp.kron(jnp.eye(reps, dtype=w.dtype), jnp.ones((seq, seq), w.dtype))
    wm = jnp.tile(w[:, :seq, :seq], (1, reps, reps)) * blockdiag
    return wm.astype(BF16), jnp.tile(b_s[:, :seq].T, (reps, 1))


def kernel(x_prompt, x_sample, mem_prompt, state_gla, cache_conv_c, cache_ffn_conv, cache_mem_k, cache_mem_v,
           norm_g, mem_norm_g, final_g,
           sg_w_in, sg_ln_g, sg_w_s, sg_b_s, sg_w_out,
           gla_w_in, gla_w_gate, gla_b_gate, gla_norm_g, gla_w_out,
           sc_w_in, sc_conv, sc_w_out,
           mem_w_q, mem_w_kv, mem_w_o,
           ffn_w_up, ffn_conv, ffn_w_down):
    depth = norm_g.shape[0]
    d_model = x_prompt.shape[-1]
    d_ff = ffn_conv.shape[-1]
    d_ffp = -(-d_ff // FFN_PAD) * FFN_PAD
    gla_heads, gla_dk, gla_dv = state_gla.shape[2], state_gla.shape[3], state_gla.shape[4]
    gla_main = 2 * gla_heads * (gla_dk + gla_dv)
    mem_heads = cache_mem_k.shape[3]
    mem_len = cache_mem_k.shape[2]
    row = lambda a: a.reshape(1, -1)

    def cast(w, **kw):
        kw.setdefault("tr", min(w.shape[1], CAST_TILE))
        kw.setdefault("tc", min(w.shape[2], CAST_TILE))
        return _cast_weights(w, **kw)

    w_sg_in, w_sg_out = cast(sg_w_in), cast(sg_w_out)
    w_gla_in = cast(gla_w_in, group_cols=gla_main)
    w_gla_lr = cast(gla_w_in, col_offset=gla_main, group_cols=gla_w_in.shape[2] - gla_main, out_group_cols=LANES,
                    tr=d_model, tc=LANES)
    w_gla_gate = [jnp.pad(gla_w_gate[b].astype(BF16), ((0, LANES - gla_w_gate.shape[1]), (0, 0)))
                  for b in range(gla_w_gate.shape[0])]
    w_gla_out = cast(gla_w_out)
    w_sc_in, w_sc_out = cast(sc_w_in), cast(sc_w_out)
    w_q, w_kv, w_o = cast(mem_w_q), cast(mem_w_kv), cast(mem_w_o)
    w_up = cast(ffn_w_up, groups=2, group_cols=d_ff, out_group_cols=d_ffp, tr=d_model, tc=FFN_CAST_TILE)
    w_down = cast(ffn_w_down, out_rows=d_ffp, tr=FFN_CAST_TILE, tc=d_model)
    ffn_cw = [_pad_last(ffn_conv[l], d_ffp) for l in range(depth)]

    def run(x, seq, mk, mv, s_gla, buf_c, buf_f, emit_v):
        m = x.shape[0]
        t = _tiles(m)
        sg_v, gla_out, c_out, f_out = [], [], [], []
        ia = ib = ic = 0
        hn = _rms_cast(x, row(norm_g[0, 0]), tm=t["norm"])
        for l in range(depth):
            kind = l % N_MIXERS
            if kind == 0:
                z = _proj(hn, w_sg_in, ia, act="gelu", tm=t["proj"], tn=PROJ_TN)
                wm, bias = _sg_weights(sg_w_s[ia], sg_b_s[ia], seq)
                lhs, v_rows = _sg_mix(z, row(sg_ln_g[ia]), wm, bias, emit_v=emit_v)
                sg_v.append(v_rows)
                w_out, i_out = w_sg_out, ia
                ia += 1
            elif kind == 1:
                proj = _proj(hn, w_gla_in, ib, tm=t["proj"], tn=PROJ_TN)
                glr = _proj(hn, w_gla_lr, ib, tm=t["proj"], tn=LANES)
                lhs, s_new = _gla_core(proj, glr, w_gla_gate[ib], row(gla_b_gate[ib]), row(gla_norm_g[ib]),
                                       s_gla[ib], seq=seq, heads=gla_heads)
                gla_out.append(s_new)
                w_out, i_out = w_gla_out, ib
                ib += 1
            else:
                lhs, cb = _proj_conv_gate(hn, w_sc_in, ic, sc_conv[ic], buf_c[ic], mode="sc", seq=seq,
                                          tm=t["sc"], tn=PROJ_TN)
                c_out.append(cb)
                w_out, i_out = w_sc_out, ic
                ic += 1
            x, hn = _outproj(lhs, w_out, i_out, x, row(norm_g[l, 1]), row(norm_g[l, 2]), hn_dtype=BF16, emit_x=True,
                             tm=t["out_resident"], tk=None)
            att = _q_attn(hn, w_q, mk, mv, l, seq=seq, heads=mem_heads, tm=t["attn"])
            x, hn = _outproj(att, w_o, l, x, row(norm_g[l, 3]), row(norm_g[l, 4]), hn_dtype=BF16, emit_x=True,
                             tm=t["out_resident"], tk=None)
            hid, fb = _proj_conv_gate(hn, w_up, l, ffn_cw[l], _pad_last(buf_f[l], d_ffp), mode="ffn", seq=seq,
                                      tm=t["proj"], tn=PROJ_TN)
            f_out.append(fb[:, :, :d_ff])
            if l < depth - 1:
                x, hn = _outproj(hid, w_down, l, x, row(norm_g[l, 5]), row(norm_g[l + 1, 0]), hn_dtype=BF16,
                                 emit_x=True, tm=t["out"], tk=OUT_TK)
            else:
                hn, = _outproj(hid, w_down, l, x, row(norm_g[l, 5]), row(final_g), hn_dtype=F32,
                               emit_x=False, tm=t["out"], tk=OUT_TK)
        return hn, sg_v, gla_out, c_out, f_out

    bp, sp, _ = x_prompt.shape
    bs, ss, _ = x_sample.shape

    mem2d = mem_prompt.reshape(bp * mem_len, d_model)
    mem_k_p, mem_v_p = _mem_kv(mem2d, mem_norm_g.reshape(depth, 1, d_model), w_kv, tm=512, tn=512)
    mem_k_p = mem_k_p.reshape(depth, bp, mem_len, d_model)
    mem_v_p = mem_v_p.reshape(depth, bp, mem_len, d_model)
    y_p, _, gla_p, conv_p, ffn_p = run(
        x_prompt.reshape(bp * sp, d_model), sp, mem_k_p, mem_v_p,
        jnp.zeros((state_gla.shape[0], bp) + state_gla.shape[2:], F32),
        jnp.zeros((cache_conv_c.shape[0], bp, 2, d_model), F32),
        jnp.zeros((depth, bp, 2, d_ff), F32), emit_v=False)

    y_s, sgv_s, gla_s, conv_s, ffn_s = run(
        x_sample.reshape(bs * ss, d_model), ss,
        cache_mem_k.reshape(depth, bs, mem_len, -1), cache_mem_v.reshape(depth, bs, mem_len, -1),
        state_gla, cache_conv_c, cache_ffn_conv, emit_v=True)

    kv_shape = (depth, bp, mem_len, mem_heads, d_model // mem_heads)
    return (y_p.reshape(bp, sp, d_model), y_s.reshape(bs, ss, d_model),
            jnp.stack(gla_p), jnp.stack(conv_p), jnp.stack(ffn_p),
            mem_k_p.reshape(kv_shape), mem_v_p.reshape(kv_shape),
            jnp.stack([v.reshape(bs, ss, -1) for v in sgv_s]), jnp.stack(gla_s), jnp.stack(conv_s), jnp.stack(ffn_s))
```

```python
import functools

import jax
import jax.numpy as jnp
from jax import lax
from jax.experimental import pallas as pl
from jax.experimental.pallas import tpu as pltpu

F32 = jnp.float32
BF16 = jnp.bfloat16

EPS = 1e-6
CHUNK = 64
GLA_TAU = 16.0
GLA_BLOCK = 64
N_MIXERS = 3

V7X_VMEM_LIMIT_BYTES = 56 * 1024 * 1024
V7X_VMEM_LIMIT_RESIDENT_BYTES = 60 * 1024 * 1024
LANES = 128
FFN_PAD = 1024
PCG_ROW_CHUNK = 256
ATTN_ROW_CHUNK = 512
PROJ_TN = 512
PLAIN_TN = 1024
OUT_TK = 1024
CAST_TILE = 1024
FFN_CAST_TILE = 256


def _cparams(n_axes, vmem_limit=V7X_VMEM_LIMIT_BYTES):
    return pltpu.CompilerParams(dimension_semantics=("arbitrary",) * n_axes, vmem_limit_bytes=vmem_limit)


def _dot(a, b):
    return jnp.dot(a, b, preferred_element_type=F32)


def _rms(x):
    return x * lax.rsqrt(jnp.mean(x * x, axis=-1, keepdims=True) + EPS)


def _sigmoid(x):
    return 1.0 / (1.0 + jnp.exp(-x))


def _rms_cast_kernel(x_ref, g_ref, o_ref):
    o_ref[...] = (_rms(x_ref[...]) * g_ref[...]).astype(o_ref.dtype)


def _rms_cast(x, g, *, tm):
    m, d = x.shape
    return pl.pallas_call(
        _rms_cast_kernel,
        grid=(m // tm,),
        in_specs=[pl.BlockSpec((tm, d), lambda i: (i, 0)), pl.BlockSpec((1, d), lambda i: (0, 0))],
        out_specs=pl.BlockSpec((tm, d), lambda i: (i, 0)),
        out_shape=jax.ShapeDtypeStruct((m, d), BF16),
        compiler_params=_cparams(1),
        name="rms_cast",
    )(x, g)


def _proj_kernel(h_ref, w_ref, o_ref, *, act):
    tm = h_ref.shape[0]
    rc = min(tm, PCG_ROW_CHUNK)
    for c in range(tm // rc):
        rows = slice(c * rc, (c + 1) * rc)
        acc = _dot(h_ref[rows, :], w_ref[...])
        if act == "gelu":
            acc = 0.5 * acc * (1.0 + jnp.tanh(0.7978845608028654 * (acc + 0.044715 * (acc * acc * acc))))
        o_ref[rows, :] = acc.astype(o_ref.dtype)


def _proj(h, w, layer, *, act=None, tm, tn):
    m, k = h.shape
    n = w.shape[2]
    return pl.pallas_call(
        functools.partial(_proj_kernel, act=act),
        grid=(n // tn, m // tm),
        in_specs=[pl.BlockSpec((tm, k), lambda j, i: (i, 0)), pl.BlockSpec((None, k, tn), lambda j, i: (layer, 0, j))],
        out_specs=pl.BlockSpec((tm, tn), lambda j, i: (i, j)),
        out_shape=jax.ShapeDtypeStruct((m, n), BF16),
        compiler_params=_cparams(2),
        name="proj_" + (act or "plain"),
    )(h, w)


def _cast_kernel(src_ref, dst_ref, *, src_rows, group_cols, masked):
    v = src_ref[...]
    if masked:
        tr, tc = v.shape
        rows = lax.broadcasted_iota(jnp.int32, (tr, 1), 0) + pl.program_id(1) * tr
        cols = lax.broadcasted_iota(jnp.int32, (1, tc), 1) + pl.program_id(3) * tc
        v = jnp.where((rows < src_rows) & (cols < group_cols), v, 0.0)
    dst_ref[...] = v.astype(dst_ref.dtype)


def _cast_weights(src, *, out_rows=None, out_group_cols=None, groups=1, group_cols=None, col_offset=0, tr, tc):
    nl, src_rows, src_cols = src.shape
    out_rows = out_rows or src_rows
    group_cols = group_cols or src_cols
    out_group_cols = out_group_cols or group_cols
    assert out_rows % tr == 0 and out_group_cols % tc == 0 and col_offset % tc == 0
    assert groups == 1 or group_cols % tc == 0
    masked = out_rows != src_rows or out_group_cols != group_cols
    last_rb = (src_rows - 1) // tr
    last_cb = (src_cols - 1) // tc
    ncb = out_group_cols // tc
    src_map = lambda l, r, g, c: (l, jnp.minimum(r, last_rb),
                                  jnp.minimum((col_offset + g * group_cols) // tc + c, last_cb))
    return pl.pallas_call(
        functools.partial(_cast_kernel, src_rows=src_rows, group_cols=group_cols, masked=masked),
        grid=(nl, out_rows // tr, groups, ncb),
        in_specs=[pl.BlockSpec((None, tr, tc), src_map)],
        out_specs=pl.BlockSpec((None, tr, tc), lambda l, r, g, c: (l, r, g * ncb + c)),
        out_shape=jax.ShapeDtypeStruct((nl, out_rows, groups * out_group_cols), BF16),
        compiler_params=_cparams(4),
        name="cast_weights",
    )(src)


def _pcg_kernel(*refs, mode, tm, seq, nseg):
    if mode == "ffn":
        h_ref, wz_ref, wg_ref, cw_ref, pre_ref, o_ref, cache_ref, scr = refs
    else:
        h_ref, wg_ref, wz_ref, wx_ref, cw_ref, pre_ref, o_ref, cache_ref, scr = refs
    i = pl.program_id(1)
    cw = cw_ref[...]

    if nseg == 1:
        first = (i % (seq // tm)) == 0

        @pl.when(first)
        def _():
            scr[6:8, :] = pre_ref[0]

        @pl.when(jnp.logical_not(first))
        def _():
            scr[6:8, :] = scr[tm + 6:tm + 8, :]
    else:
        scr[6:8, :] = pre_ref[0]

    rc = min(tm, PCG_ROW_CHUNK)
    for c in range(tm // rc):
        r0 = c * rc
        hv = h_ref[r0:r0 + rc, :]
        zc = _dot(hv, wz_ref[...])
        if mode == "sc":
            zc = zc * _dot(hv, wx_ref[...])
        gate = _dot(hv, wg_ref[...])
        scr[8 + r0:8 + r0 + rc, :] = zc
        sh1 = scr[7 + r0:7 + r0 + rc, :]
        sh2 = scr[6 + r0:6 + r0 + rc, :]
        if nseg > 1:
            row = lax.broadcasted_iota(jnp.int32, (rc, 1), 0) + r0
            for s in range(nseg):
                p = pre_ref[s]
                sh1 = jnp.where(row == s * seq, p[1:2, :], sh1)
                sh2 = jnp.where(row == s * seq, p[0:1, :], sh2)
                sh2 = jnp.where(row == s * seq + 1, p[1:2, :], sh2)
        y = cw[0:1, :] * sh2 + cw[1:2, :] * sh1 + cw[2:3, :] * zc
        if mode == "ffn":
            out = (y * _sigmoid(y)) * gate
        else:
            out = gate * y
        o_ref[r0:r0 + rc, :] = out.astype(o_ref.dtype)
    if nseg == 1:
        cache_ref[0] = scr[tm + 6:tm + 8, :]
    else:
        for s in range(nseg):
            cache_ref[s] = scr[8 + (s + 1) * seq - 2:8 + (s + 1) * seq, :]


def _proj_conv_gate(h, w, layer, conv_w, prefix, *, mode, seq, tm, tn):
    m, k = h.shape
    width = conv_w.shape[1]
    nj = width // tn
    assert seq >= 2 and (seq % tm == 0 or tm % seq == 0)
    nseg = max(1, tm // seq)
    if nseg == 1:
        tiles_per_seq = seq // tm
        pre_map = lambda j, i: (i // tiles_per_seq, 0, j)
    else:
        pre_map = lambda j, i: (i, 0, j)
    n_groups = 2 if mode == "ffn" else 3
    w_specs = [pl.BlockSpec((None, k, tn), functools.partial(lambda j, i, g: (layer, 0, j + g * nj), g=g))
               for g in range(n_groups)]
    nbatch = m // seq
    out, cache = pl.pallas_call(
        functools.partial(_pcg_kernel, mode=mode, tm=tm, seq=seq, nseg=nseg),
        grid=(nj, m // tm),
        in_specs=[pl.BlockSpec((tm, k), lambda j, i: (i, 0))] + w_specs + [
            pl.BlockSpec((3, tn), lambda j, i: (0, j)),
            pl.BlockSpec((nseg, 2, tn), pre_map),
        ],
        out_specs=[pl.BlockSpec((tm, tn), lambda j, i: (i, j)), pl.BlockSpec((nseg, 2, tn), pre_map)],
        out_shape=[jax.ShapeDtypeStruct((m, width), BF16), jax.ShapeDtypeStruct((nbatch, 2, width), F32)],
        scratch_shapes=[pltpu.VMEM((tm + 8, tn), F32)],
        compiler_params=_cparams(2),
        name="proj_conv_gate_" + mode,
    )(h, *([w] * n_groups), conv_w, prefix)
    return out, cache


def _sg_mix_kernel(zu_ref, zv_ref, lng_ref, wm_ref, bias_ref, o_ref, *v_refs, heads):
    vpre = zv_ref[...].astype(F32)
    xc = vpre - jnp.mean(vpre, axis=-1, keepdims=True)
    v = xc * lax.rsqrt(jnp.mean(xc * xc, axis=-1, keepdims=True) + EPS) * lng_ref[...]
    if v_refs:
        v_refs[0][...] = v
    vb = v.astype(BF16)
    hd = vb.shape[1] // heads
    bias = bias_ref[...]
    for h in range(heads):
        mixed = _dot(wm_ref[h], vb[:, h * hd:(h + 1) * hd]) + bias[:, h:h + 1]
        o_ref[:, h * hd:(h + 1) * hd] = (zu_ref[:, h * hd:(h + 1) * hd].astype(F32) * mixed).astype(o_ref.dtype)


def _sg_mix(z, ln_g, wm, bias, *, emit_v):
    m = z.shape[0]
    width = z.shape[1] // 2
    heads, ln = wm.shape[0], wm.shape[1]
    out_shape = [jax.ShapeDtypeStruct((m, width), BF16)]
    out_specs = [pl.BlockSpec((ln, width), lambda c: (c, 0))]
    if emit_v:
        out_shape.append(jax.ShapeDtypeStruct((m, width), F32))
        out_specs.append(pl.BlockSpec((ln, width), lambda c: (c, 0)))
    res = pl.pallas_call(
        functools.partial(_sg_mix_kernel, heads=heads),
        grid=(m // ln,),
        in_specs=[
            pl.BlockSpec((ln, width), lambda c: (c, 0)),
            pl.BlockSpec((ln, width), lambda c: (c, 1)),
            pl.BlockSpec((1, width), lambda c: (0, 0)),
            pl.BlockSpec((heads, ln, ln), lambda c: (0, 0, 0)),
            pl.BlockSpec((ln, heads), lambda c: (0, 0)),
        ],
        out_specs=out_specs,
        out_shape=out_shape,
        compiler_params=_cparams(1),
        name="sg_mix",
    )(z, z, ln_g, wm, bias)
    return (res[0], res[1]) if emit_v else (res[0], None)


def _gla_kernel(q_ref, k_ref, v_ref, r_ref, glr_ref, wg_ref, bg_ref, ng_ref, s0_ref, o_ref, sout_ref, s_scr,
                *, blk, nsub, nstep, scale):
    n = pl.program_id(2)
    rows = blk * nsub
    assert nsub in (1, 2)

    @pl.when(n == 0)
    def _():
        s_scr[...] = s0_ref[...]

    pre = _dot(glr_ref[...], wg_ref[...]) + bg_ref[...]
    la = (jnp.minimum(pre, 0.0) - jnp.log1p(jnp.exp(-jnp.abs(pre)))) * (1.0 / GLA_TAU)
    ri = lax.broadcasted_iota(jnp.int32, (rows, rows), 0)
    ci = lax.broadcasted_iota(jnp.int32, (rows, rows), 1)
    causal = ri >= ci
    if nsub == 2:
        cross = (ri >= blk) & (ci < blk)
        causal = causal & jnp.logical_not(cross)
    tri = jnp.where(causal, 1.0, 0.0).astype(BF16)
    la1 = la.astype(BF16)
    rem = la - la1.astype(F32)
    la2 = rem.astype(BF16)
    la3 = (rem - la2.astype(F32)).astype(BF16)
    g = _dot(tri, la1) + _dot(tri, la2) + _dot(tri, la3)
    g_a = g[blk - 1:blk, :]
    q = q_ref[...].astype(F32) * scale
    k = k_ref[...].astype(F32)
    q_t = q * jnp.exp(g)
    k_t = (k * jnp.exp(-g)).astype(BF16)
    if nsub == 1:
        k_end = k * jnp.exp(g_a - g)
        q_inter, k_upd, log_decay = q_t, k_end, g_a
    else:
        g_b = g[rows - 1:rows, :]
        in_b = lax.broadcasted_iota(jnp.int32, (rows, 1), 0) >= blk
        k_end = k * jnp.exp(jnp.where(in_b, g_b, g_a) - g)
        q_inter = jnp.where(in_b, q_t * jnp.exp(g_a), q_t)
        k_upd = jnp.where(in_b, k_end, k_end * jnp.exp(g_b))
        log_decay = g_a + g_b
    q_t = q_t.astype(BF16)
    nt = (((1,), (1,)), ((), ()))
    att = jnp.where(causal, lax.dot_general(q_t, k_t, nt, preferred_element_type=F32), 0.0)
    if nsub == 2:
        att = jnp.where(cross, lax.dot_general(q_t, k_end.astype(BF16), nt, preferred_element_type=F32), att)
    v = v_ref[...]
    s = s_scr[...]
    o = _dot(att.astype(BF16), v) + _dot(q_inter.astype(BF16), s.astype(BF16))
    decay = jnp.transpose(jnp.broadcast_to(jnp.exp(log_decay), (LANES, g.shape[1])))[:, 0:1]
    s_new = decay * s + lax.dot_general(k_upd.astype(BF16), v, (((0,), (0,)), ((), ())),
                                        preferred_element_type=F32)
    s_scr[...] = s_new

    @pl.when(n == nstep - 1)
    def _():
        sout_ref[...] = s_new

    r = r_ref[...].astype(F32)
    o_ref[...] = (_rms(o) * ng_ref[...] * (r * _sigmoid(r))).astype(o_ref.dtype)


def _gla_core(proj, glr, w_gate, b_gate, norm_g, s0, *, seq, heads):
    m = proj.shape[0]
    nbatch = m // seq
    dk, dv = s0.shape[2], s0.shape[3]
    blk = GLA_BLOCK if seq % GLA_BLOCK == 0 else seq
    nsub = 2 if (seq // blk) % 2 == 0 else 1
    nstep = seq // (blk * nsub)
    rows = blk * nsub
    kq = heads * dk // dk
    row = lambda b, h, n: b * nstep + n
    out, s_new = pl.pallas_call(
        functools.partial(_gla_kernel, blk=blk, nsub=nsub, nstep=nstep, scale=float(dk) ** -0.5),
        grid=(nbatch, heads, nstep),
        in_specs=[
            pl.BlockSpec((rows, dk), lambda b, h, n: (row(b, h, n), h)),
            pl.BlockSpec((rows, dk), lambda b, h, n: (row(b, h, n), kq + h)),
            pl.BlockSpec((rows, dv), lambda b, h, n: (row(b, h, n), (2 * heads * dk) // dv + h)),
            pl.BlockSpec((rows, dv), lambda b, h, n: (row(b, h, n), (2 * heads * dk) // dv + heads + h)),
            pl.BlockSpec((rows, LANES), lambda b, h, n: (row(b, h, n), 0)),
            pl.BlockSpec((LANES, dk), lambda b, h, n: (0, h)),
            pl.BlockSpec((1, dk), lambda b, h, n: (0, h)),
            pl.BlockSpec((1, dv), lambda b, h, n: (0, 0)),
            pl.BlockSpec((None, None, dk, dv), lambda b, h, n: (b, h, 0, 0)),
        ],
        out_specs=[
            pl.BlockSpec((rows, dv), lambda b, h, n: (row(b, h, n), h)),
            pl.BlockSpec((None, None, dk, dv), lambda b, h, n: (b, h, 0, 0)),
        ],
        out_shape=[jax.ShapeDtypeStruct((m, heads * dv), BF16), jax.ShapeDtypeStruct(s0.shape, F32)],
        scratch_shapes=[pltpu.VMEM((dk, dv), F32)],
        compiler_params=_cparams(3),
        name="gla_core",
    )(proj, proj, proj, proj, glr, w_gate, b_gate, norm_g, s0)
    return out, s_new


def _qattn_kernel(h_ref, wq_ref, k_ref, v_ref, o_ref, *scratch, ts, nb, scale):
    kb = k_ref[...].astype(BF16)
    vb = v_ref[...].astype(BF16)

    def attend(q):
        sc = lax.dot_general(q, kb, (((1,), (1,)), ((), ())), preferred_element_type=F32) * scale
        p = jnp.exp(sc - jnp.max(sc, axis=-1, keepdims=True))
        p = p / jnp.sum(p, axis=-1, keepdims=True)
        return _dot(p.astype(BF16), vb).astype(o_ref.dtype)

    if nb == 1:
        rc = min(ts, ATTN_ROW_CHUNK)
        for c in range(ts // rc):
            rows = slice(c * rc, (c + 1) * rc)
            o_ref[rows, :] = attend(_dot(h_ref[rows, :], wq_ref[...]).astype(BF16))
    else:
        q_scr, = scratch
        s = pl.program_id(2)

        @pl.when(s == 0)
        def _():
            q_scr[...] = _dot(h_ref[...], wq_ref[...]).astype(BF16)

        o_ref[...] = attend(q_scr[pl.ds(pl.multiple_of(s * ts, ts), ts), :])


def _q_attn(h, wq, mk, mv, layer, *, seq, heads, tm):
    m, k = h.shape
    hd = wq.shape[2] // heads
    mem_len = mk.shape[2]
    ts = min(tm, seq)
    nb = tm // ts
    tiles_per_seq = seq // ts
    kv_map = lambda i, hh, s: (layer, (i * nb + s) // tiles_per_seq, 0, hh)
    return pl.pallas_call(
        functools.partial(_qattn_kernel, ts=ts, nb=nb, scale=float(hd) ** -0.5),
        grid=(m // tm, heads, nb),
        in_specs=[
            pl.BlockSpec((tm, k), lambda i, hh, s: (i, 0)),
            pl.BlockSpec((None, k, hd), lambda i, hh, s: (layer, 0, hh)),
            pl.BlockSpec((None, None, mem_len, hd), kv_map),
            pl.BlockSpec((None, None, mem_len, hd), kv_map),
        ],
        out_specs=pl.BlockSpec((ts, hd), lambda i, hh, s: (i * nb + s, hh)),
        out_shape=jax.ShapeDtypeStruct((m, heads * hd), BF16),
        scratch_shapes=[pltpu.VMEM((tm, hd), BF16)] if nb > 1 else [],
        compiler_params=_cparams(3),
        name="q_attn",
    )(h, wq, mk, mv)


def _mem_kv_kernel(mem_ref, g_ref, wk_ref, wv_ref, k_ref, v_ref, hn_scr):
    @pl.when(pl.program_id(2) == 0)
    def _():
        hn_scr[...] = (_rms(mem_ref[...]) * g_ref[...]).astype(BF16)

    hn = hn_scr[...]
    k_ref[...] = _dot(hn, wk_ref[...])
    v_ref[...] = _dot(hn, wv_ref[...])


def _mem_kv(mem, g, w_kv, *, tm, tn):
    m, d = mem.shape
    depth = w_kv.shape[0]
    n = w_kv.shape[2] // 2
    nj = n // tn
    return pl.pallas_call(
        _mem_kv_kernel,
        grid=(depth, m // tm, nj),
        in_specs=[
            pl.BlockSpec((tm, d), lambda l, i, j: (i, 0)),
            pl.BlockSpec((None, 1, d), lambda l, i, j: (l, 0, 0)),
            pl.BlockSpec((None, d, tn), lambda l, i, j: (l, 0, j)),
            pl.BlockSpec((None, d, tn), lambda l, i, j: (l, 0, j + nj)),
        ],
        out_specs=[pl.BlockSpec((None, tm, tn), lambda l, i, j: (l, i, j)),
                   pl.BlockSpec((None, tm, tn), lambda l, i, j: (l, i, j))],
        out_shape=[jax.ShapeDtypeStruct((depth, m, n), F32), jax.ShapeDtypeStruct((depth, m, n), F32)],
        scratch_shapes=[pltpu.VMEM((tm, d), BF16)],
        compiler_params=_cparams(3),
        name="mem_kv",
    )(mem, g, w_kv, w_kv)


OUT_COL_CHUNK = 1024
OUT_EPI_ROWS = 64
OUT_ROW_GROUP = 128


def _accumulate(lhs_ref, w_ref, acc_ref, first):
    lhs = lhs_ref[...]
    d = acc_ref.shape[1]
    tn = min(d, OUT_COL_CHUNK)
    for c in range(d // tn):
        cs = slice(c * tn, (c + 1) * tn)
        part = _dot(lhs, w_ref[:, cs])
        acc_ref[:, cs] = part if first else acc_ref[:, cs] + part


def _res_norm_tail(acc_ref, x_ref, gpost_ref, gnext_ref, xo_ref, hn_ref, *, unrolled=False):
    gpost = gpost_ref[...]
    gnext = gnext_ref[...]
    rows = min(acc_ref.shape[0], OUT_EPI_ROWS)

    def chunk(sl):
        xn = x_ref[sl, :] + _rms(acc_ref[sl, :]) * gpost
        if xo_ref is not None:
            xo_ref[sl, :] = xn
        hn_ref[sl, :] = (_rms(xn) * gnext).astype(hn_ref.dtype)

    if unrolled:
        for c in range(acc_ref.shape[0] // rows):
            chunk(slice(c * rows, (c + 1) * rows))
    else:
        def body(c, carry):
            chunk(pl.ds(pl.multiple_of(c * rows, rows), rows))
            return carry

        lax.fori_loop(0, acc_ref.shape[0] // rows, body, 0)


def _outproj_ktiled_kernel(lhs_ref, w_ref, x_hbm, gpost_ref, gnext_ref, *rest, nk, emit_x):
    if emit_x:
        xo_ref, hn_ref, x_buf, x_sem = rest
        acc_ref = xo_ref
    else:
        hn_ref, acc_ref, x_buf, x_sem = rest
        xo_ref = None
    tm = x_buf.shape[0]
    row0 = pl.multiple_of(pl.program_id(0) * tm, tm)
    x_copy = pltpu.make_async_copy(x_hbm.at[pl.ds(row0, tm), :], x_buf, x_sem)
    kk = pl.program_id(1)

    @pl.when(kk == 0)
    def _():
        x_copy.start()
        _accumulate(lhs_ref, w_ref, acc_ref, True)

    pl.when(kk > 0)(functools.partial(_accumulate, lhs_ref, w_ref, acc_ref, False))

    @pl.when(kk == nk - 1)
    def _():
        x_copy.wait()
        _res_norm_tail(acc_ref, x_buf, gpost_ref, gnext_ref, xo_ref, hn_ref)


def _outproj_resident_kernel(lhs_ref, w_ref, x_ref, gpost_ref, gnext_ref, xo_ref, hn_ref):
    tm = xo_ref.shape[0]
    group = min(tm, OUT_ROW_GROUP)
    for r in range(tm // group):
        rows = slice(r * group, (r + 1) * group)
        _accumulate(lhs_ref.at[rows], w_ref, xo_ref.at[rows], True)
        _res_norm_tail(xo_ref.at[rows], x_ref.at[rows], gpost_ref, gnext_ref, xo_ref.at[rows], hn_ref.at[rows],
                       unrolled=True)


def _outproj(lhs, w, layer, x, g_post, g_next, *, hn_dtype, emit_x, tm, tk):
    m, kdim = lhs.shape
    d = w.shape[2]
    out_shape = [jax.ShapeDtypeStruct((m, d), hn_dtype)]
    if emit_x:
        out_shape.insert(0, jax.ShapeDtypeStruct((m, d), F32))
    if tk is None:
        assert emit_x
        row_map = lambda i: (i, 0)
        fix_map = lambda i: (0, 0)
        return pl.pallas_call(
            _outproj_resident_kernel,
            grid=(m // tm,),
            in_specs=[
                pl.BlockSpec((tm, kdim), row_map),
                pl.BlockSpec((None, kdim, d), lambda i: (layer, 0, 0), pipeline_mode=pl.Buffered(1)),
                pl.BlockSpec((tm, d), row_map),
                pl.BlockSpec((1, d), fix_map),
                pl.BlockSpec((1, d), fix_map),
            ],
            out_specs=[pl.BlockSpec((tm, d), row_map), pl.BlockSpec((tm, d), row_map)],
            out_shape=out_shape,
            compiler_params=_cparams(1, V7X_VMEM_LIMIT_RESIDENT_BYTES),
            name="outproj_resident",
        )(lhs, w, x, g_post, g_next)
    nk = kdim // tk
    row_map = lambda i, kk: (i, 0)
    fix_map = lambda i, kk: (0, 0)
    return pl.pallas_call(
        functools.partial(_outproj_ktiled_kernel, nk=nk, emit_x=emit_x),
        grid=(m // tm, nk),
        in_specs=[
            pl.BlockSpec((tm, tk), lambda i, kk: (i, kk)),
            pl.BlockSpec((None, tk, d), lambda i, kk: (layer, kk, 0)),
            pl.BlockSpec(memory_space=pl.ANY),
            pl.BlockSpec((1, d), fix_map),
            pl.BlockSpec((1, d), fix_map),
        ],
        out_specs=[pl.BlockSpec((tm, d), row_map)] * len(out_shape),
        out_shape=out_shape,
        scratch_shapes=([] if emit_x else [pltpu.VMEM((tm, d), F32)])
        + [pltpu.VMEM((tm, d), F32), pltpu.SemaphoreType.DMA(())],
        compiler_params=_cparams(2),
        name="outproj_ktiled",
    )(lhs, w, x, g_post, g_next)


def _pad_last(a, n):
    return jnp.pad(a, [(0, 0)] * (a.ndim - 1) + [(0, n - a.shape[-1])])


def _tiles(m):
    big = m >= 1024
    return dict(
        norm=256 if big else m,
        proj=1024 if big else m,
        sc=512 if big else m,
        attn=1024 if big else m,
        out=512 if big else m,
        out_resident=256 if big else m,
    )


def _sg_weights(w_s, b_s, seq):
    heads, ln, _ = w_s.shape
    pos = jnp.arange(ln) // CHUNK
    w = jnp.where(pos[None, :] <= pos[:, None], w_s, jnp.zeros_like(w_s))
    if seq % ln == 0:
        return w.astype(BF16), b_s.T
    assert ln % seq == 0
    reps = ln // seq
    blockdiag = jnp.kron(jnp.eye(reps, dtype=w.dtype), jnp.ones((seq, seq), w.dtype))
    wm = jnp.tile(w[:, :seq, :seq], (1, reps, reps)) * blockdiag
    return wm.astype(BF16), jnp.tile(b_s[:, :seq].T, (reps, 1))


def kernel(x_prompt, x_sample, mem_prompt, state_gla, cache_conv_c, cache_ffn_conv, cache_mem_k, cache_mem_v,
           norm_g, mem_norm_g, final_g,
           sg_w_in, sg_ln_g, sg_w_s, sg_b_s, sg_w_out,
           gla_w_in, gla_w_gate, gla_b_gate, gla_norm_g, gla_w_out,
           sc_w_in, sc_conv, sc_w_out,
           mem_w_q, mem_w_kv, mem_w_o,
           ffn_w_up, ffn_conv, ffn_w_down):
    depth = norm_g.shape[0]
    d_model = x_prompt.shape[-1]
    d_ff = ffn_conv.shape[-1]
    d_ffp = -(-d_ff // FFN_PAD) * FFN_PAD
    gla_heads, gla_dk, gla_dv = state_gla.shape[2], state_gla.shape[3], state_gla.shape[4]
    gla_main = 2 * gla_heads * (gla_dk + gla_dv)
    mem_heads = cache_mem_k.shape[3]
    mem_len = cache_mem_k.shape[2]
    row = lambda a: a.reshape(1, -1)

    def cast(w, **kw):
        kw.setdefault("tr", min(w.shape[1], CAST_TILE))
        kw.setdefault("tc", min(w.shape[2], CAST_TILE))
        return _cast_weights(w, **kw)

    w_sg_in, w_sg_out = cast(sg_w_in), cast(sg_w_out)
    w_gla_in = cast(gla_w_in, group_cols=gla_main)
    w_gla_lr = cast(gla_w_in, col_offset=gla_main, group_cols=gla_w_in.shape[2] - gla_main, out_group_cols=LANES,
                    tr=d_model, tc=LANES)
    w_gla_gate = [jnp.pad(gla_w_gate[b].astype(BF16), ((0, LANES - gla_w_gate.shape[1]), (0, 0)))
                  for b in range(gla_w_gate.shape[0])]
    w_gla_out = cast(gla_w_out)
    w_sc_in, w_sc_out = cast(sc_w_in), cast(sc_w_out)
    w_q, w_kv, w_o = cast(mem_w_q), cast(mem_w_kv), cast(mem_w_o)
    w_up = cast(ffn_w_up, groups=2, group_cols=d_ff, out_group_cols=d_ffp, tr=d_model, tc=FFN_CAST_TILE)
    w_down = cast(ffn_w_down, out_rows=d_ffp, tr=FFN_CAST_TILE, tc=d_model)
    ffn_cw = [_pad_last(ffn_conv[l], d_ffp) for l in range(depth)]

    def run(x, seq, mk, mv, s_gla, buf_c, buf_f, emit_v):
        m = x.shape[0]
        t = _tiles(m)
        sg_v, gla_out, c_out, f_out = [], [], [], []
        ia = ib = ic = 0
        hn = _rms_cast(x, row(norm_g[0, 0]), tm=t["norm"])
        for l in range(depth):
            kind = l % N_MIXERS
            if kind == 0:
                z = _proj(hn, w_sg_in, ia, act="gelu", tm=t["proj"], tn=PLAIN_TN)
                wm, bias = _sg_weights(sg_w_s[ia], sg_b_s[ia], seq)
                lhs, v_rows = _sg_mix(z, row(sg_ln_g[ia]), wm, bias, emit_v=emit_v)
                sg_v.append(v_rows)
                w_out, i_out = w_sg_out, ia
                ia += 1
            elif kind == 1:
                proj = _proj(hn, w_gla_in, ib, tm=t["proj"], tn=PLAIN_TN)
                glr = _proj(hn, w_gla_lr, ib, tm=t["proj"], tn=LANES)
                lhs, s_new = _gla_core(proj, glr, w_gla_gate[ib], row(gla_b_gate[ib]), row(gla_norm_g[ib]),
                                       s_gla[ib], seq=seq, heads=gla_heads)
                gla_out.append(s_new)
                w_out, i_out = w_gla_out, ib
                ib += 1
            else:
                lhs, cb = _proj_conv_gate(hn, w_sc_in, ic, sc_conv[ic], buf_c[ic], mode="sc", seq=seq,
                                          tm=t["sc"], tn=PROJ_TN)
                c_out.append(cb)
                w_out, i_out = w_sc_out, ic
                ic += 1
            x, hn = _outproj(lhs, w_out, i_out, x, row(norm_g[l, 1]), row(norm_g[l, 2]), hn_dtype=BF16, emit_x=True,
                             tm=t["out_resident"], tk=None)
            att = _q_attn(hn, w_q, mk, mv, l, seq=seq, heads=mem_heads, tm=t["attn"])
            x, hn = _outproj(att, w_o, l, x, row(norm_g[l, 3]), row(norm_g[l, 4]), hn_dtype=BF16, emit_x=True,
                             tm=t["out_resident"], tk=None)
            hid, fb = _proj_conv_gate(hn, w_up, l, ffn_cw[l], _pad_last(buf_f[l], d_ffp), mode="ffn", seq=seq,
                                      tm=t["proj"], tn=PROJ_TN)
            f_out.append(fb[:, :, :d_ff])
            if l < depth - 1:
                x, hn = _outproj(hid, w_down, l, x, row(norm_g[l, 5]), row(norm_g[l + 1, 0]), hn_dtype=BF16,
                                 emit_x=True, tm=t["out"], tk=OUT_TK)
            else:
                hn, = _outproj(hid, w_down, l, x, row(norm_g[l, 5]), row(final_g), hn_dtype=F32,
                               emit_x=False, tm=t["out"], tk=OUT_TK)
        return hn, sg_v, gla_out, c_out, f_out

    bp, sp, _ = x_prompt.shape
    bs, ss, _ = x_sample.shape

    mem2d = mem_prompt.reshape(bp * mem_len, d_model)
    mem_k_p, mem_v_p = _mem_kv(mem2d, mem_norm_g.reshape(depth, 1, d_model), w_kv, tm=512, tn=512)
    mem_k_p = mem_k_p.reshape(depth, bp, mem_len, d_model)
    mem_v_p = mem_v_p.reshape(depth, bp, mem_len, d_model)
    y_p, _, gla_p, conv_p, ffn_p = run(
        x_prompt.reshape(bp * sp, d_model), sp, mem_k_p, mem_v_p,
        jnp.zeros((state_gla.shape[0], bp) + state_gla.shape[2:], F32),
        jnp.zeros((cache_conv_c.shape[0], bp, 2, d_model), F32),
        jnp.zeros((depth, bp, 2, d_ff), F32), emit_v=False)

    y_s, sgv_s, gla_s, conv_s, ffn_s = run(
        x_sample.reshape(bs * ss, d_model), ss,
        cache_mem_k.reshape(depth, bs, mem_len, -1), cache_mem_v.reshape(depth, bs, mem_len, -1),
        state_gla, cache_conv_c, cache_ffn_conv, emit_v=True)

    kv_shape = (depth, bp, mem_len, mem_heads, d_model // mem_heads)
    return (y_p.reshape(bp, sp, d_model), y_s.reshape(bs, ss, d_model),
            jnp.stack(gla_p), jnp.stack(conv_p), jnp.stack(ffn_p),
            mem_k_p.reshape(kv_shape), mem_v_p.reshape(kv_shape),
            jnp.stack([v.reshape(bs, ss, -1) for v in sgv_s]), jnp.stack(gla_s), jnp.stack(conv_s), jnp.stack(ffn_s))
```

```python
import functools

import jax
import jax.numpy as jnp
from jax import lax
from jax.experimental import pallas as pl
from jax.experimental.pallas import tpu as pltpu

F32 = jnp.float32
BF16 = jnp.bfloat16

EPS = 1e-6
CHUNK = 64
GLA_TAU = 16.0
GLA_BLOCK = 64
N_MIXERS = 3

V7X_VMEM_LIMIT_BYTES = 56 * 1024 * 1024
V7X_VMEM_LIMIT_OUTPROJ_BYTES = 60 * 1024 * 1024
LANES = 128
FFN_PAD = 1024
PCG_ROW_CHUNK = 256
ATTN_ROW_CHUNK = 512
PROJ_TN = 512
PLAIN_TN = 1024
OUT_TK = 1024
MEM_KV_TILE = 512
CAST_TILE = 1024
FFN_CAST_TILE = 256


def _cparams(n_axes, vmem_limit=V7X_VMEM_LIMIT_BYTES):
    return pltpu.CompilerParams(dimension_semantics=("arbitrary",) * n_axes, vmem_limit_bytes=vmem_limit)


def _dot(a, b):
    return jnp.dot(a, b, preferred_element_type=F32)


def _rms(x):
    return x * lax.rsqrt(jnp.mean(x * x, axis=-1, keepdims=True) + EPS)


def _sigmoid(x):
    return 1.0 / (1.0 + jnp.exp(-x))


def _rms_cast_kernel(x_ref, g_ref, o_ref):
    o_ref[...] = (_rms(x_ref[...]) * g_ref[...]).astype(o_ref.dtype)


def _rms_cast(x, g, *, tm):
    m, d = x.shape
    return pl.pallas_call(
        _rms_cast_kernel,
        grid=(m // tm,),
        in_specs=[pl.BlockSpec((tm, d), lambda i: (i, 0)), pl.BlockSpec((1, d), lambda i: (0, 0))],
        out_specs=pl.BlockSpec((tm, d), lambda i: (i, 0)),
        out_shape=jax.ShapeDtypeStruct((m, d), BF16),
        compiler_params=_cparams(1),
        name="rms_cast",
    )(x, g)


def _proj_kernel(h_ref, w_ref, o_ref, *, act):
    tm = h_ref.shape[0]
    rc = min(tm, PCG_ROW_CHUNK)
    for c in range(tm // rc):
        rows = slice(c * rc, (c + 1) * rc)
        acc = _dot(h_ref[rows, :], w_ref[...])
        if act == "gelu":
            acc = 0.5 * acc * (1.0 + jnp.tanh(0.7978845608028654 * (acc + 0.044715 * (acc * acc * acc))))
        o_ref[rows, :] = acc.astype(o_ref.dtype)


def _proj(h, w, layer, *, act=None, tm, tn):
    m, k = h.shape
    n = w.shape[2]
    return pl.pallas_call(
        functools.partial(_proj_kernel, act=act),
        grid=(n // tn, m // tm),
        in_specs=[pl.BlockSpec((tm, k), lambda j, i: (i, 0)), pl.BlockSpec((None, k, tn), lambda j, i: (layer, 0, j))],
        out_specs=pl.BlockSpec((tm, tn), lambda j, i: (i, j)),
        out_shape=jax.ShapeDtypeStruct((m, n), BF16),
        compiler_params=_cparams(2),
        name="proj_" + (act or "plain"),
    )(h, w)


def _cast_kernel(src_ref, dst_ref, *, src_rows, group_cols, masked):
    v = src_ref[...]
    if masked:
        tr, tc = v.shape
        rows = lax.broadcasted_iota(jnp.int32, (tr, 1), 0) + pl.program_id(1) * tr
        cols = lax.broadcasted_iota(jnp.int32, (1, tc), 1) + pl.program_id(3) * tc
        v = jnp.where((rows < src_rows) & (cols < group_cols), v, 0.0)
    dst_ref[...] = v.astype(dst_ref.dtype)


def _cast_weights(src, *, out_rows=None, out_group_cols=None, groups=1, group_cols=None, col_offset=0, tr, tc):
    nl, src_rows, src_cols = src.shape
    out_rows = out_rows or src_rows
    group_cols = group_cols or src_cols
    out_group_cols = out_group_cols or group_cols
    assert out_rows % tr == 0 and out_group_cols % tc == 0 and col_offset % tc == 0
    assert groups == 1 or group_cols % tc == 0
    masked = out_rows != src_rows or out_group_cols != group_cols
    last_rb = (src_rows - 1) // tr
    last_cb = (src_cols - 1) // tc
    ncb = out_group_cols // tc
    src_map = lambda l, r, g, c: (l, jnp.minimum(r, last_rb),
                                  jnp.minimum((col_offset + g * group_cols) // tc + c, last_cb))
    return pl.pallas_call(
        functools.partial(_cast_kernel, src_rows=src_rows, group_cols=group_cols, masked=masked),
        grid=(nl, out_rows // tr, groups, ncb),
        in_specs=[pl.BlockSpec((None, tr, tc), src_map)],
        out_specs=pl.BlockSpec((None, tr, tc), lambda l, r, g, c: (l, r, g * ncb + c)),
        out_shape=jax.ShapeDtypeStruct((nl, out_rows, groups * out_group_cols), BF16),
        compiler_params=_cparams(4),
        name="cast_weights",
    )(src)


def _pcg_kernel(*refs, mode, tm, seq, nseg):
    if mode == "ffn":
        h_ref, wz_ref, wg_ref, cw_ref, pre_ref, o_ref, cache_ref, scr = refs
    else:
        h_ref, wg_ref, wz_ref, wx_ref, cw_ref, pre_ref, o_ref, cache_ref, scr = refs
    i = pl.program_id(1)
    cw = cw_ref[...]

    if nseg == 1:
        first = (i % (seq // tm)) == 0

        @pl.when(first)
        def _():
            scr[6:8, :] = pre_ref[0]

        @pl.when(jnp.logical_not(first))
        def _():
            scr[6:8, :] = scr[tm + 6:tm + 8, :]
    else:
        scr[6:8, :] = pre_ref[0]

    rc = min(tm, PCG_ROW_CHUNK)
    for c in range(tm // rc):
        r0 = c * rc
        hv = h_ref[r0:r0 + rc, :]
        zc = _dot(hv, wz_ref[...])
        if mode == "sc":
            zc = zc * _dot(hv, wx_ref[...])
        gate = _dot(hv, wg_ref[...])
        scr[8 + r0:8 + r0 + rc, :] = zc
        sh1 = scr[7 + r0:7 + r0 + rc, :]
        sh2 = scr[6 + r0:6 + r0 + rc, :]
        if nseg > 1:
            row = lax.broadcasted_iota(jnp.int32, (rc, 1), 0) + r0
            for s in range(nseg):
                p = pre_ref[s]
                sh1 = jnp.where(row == s * seq, p[1:2, :], sh1)
                sh2 = jnp.where(row == s * seq, p[0:1, :], sh2)
                sh2 = jnp.where(row == s * seq + 1, p[1:2, :], sh2)
        y = cw[0:1, :] * sh2 + cw[1:2, :] * sh1 + cw[2:3, :] * zc
        if mode == "ffn":
            out = (y * _sigmoid(y)) * gate
        else:
            out = gate * y
        o_ref[r0:r0 + rc, :] = out.astype(o_ref.dtype)
    if nseg == 1:
        cache_ref[0] = scr[tm + 6:tm + 8, :]
    else:
        for s in range(nseg):
            cache_ref[s] = scr[8 + (s + 1) * seq - 2:8 + (s + 1) * seq, :]


def _proj_conv_gate(h, w, layer, conv_w, prefix, *, mode, seq, tm, tn):
    m, k = h.shape
    width = conv_w.shape[1]
    nj = width // tn
    assert seq >= 2 and (seq % tm == 0 or tm % seq == 0)
    nseg = max(1, tm // seq)
    if nseg == 1:
        tiles_per_seq = seq // tm
        pre_map = lambda j, i: (i // tiles_per_seq, 0, j)
    else:
        pre_map = lambda j, i: (i, 0, j)
    n_groups = 2 if mode == "ffn" else 3
    w_specs = [pl.BlockSpec((None, k, tn), functools.partial(lambda j, i, g: (layer, 0, j + g * nj), g=g))
               for g in range(n_groups)]
    nbatch = m // seq
    out, cache = pl.pallas_call(
        functools.partial(_pcg_kernel, mode=mode, tm=tm, seq=seq, nseg=nseg),
        grid=(nj, m // tm),
        in_specs=[pl.BlockSpec((tm, k), lambda j, i: (i, 0))] + w_specs + [
            pl.BlockSpec((3, tn), lambda j, i: (0, j)),
            pl.BlockSpec((nseg, 2, tn), pre_map),
        ],
        out_specs=[pl.BlockSpec((tm, tn), lambda j, i: (i, j)), pl.BlockSpec((nseg, 2, tn), pre_map)],
        out_shape=[jax.ShapeDtypeStruct((m, width), BF16), jax.ShapeDtypeStruct((nbatch, 2, width), F32)],
        scratch_shapes=[pltpu.VMEM((tm + 8, tn), F32)],
        compiler_params=_cparams(2),
        name="proj_conv_gate_" + mode,
    )(h, *([w] * n_groups), conv_w, prefix)
    return out, cache


def _sg_mix_kernel(zu_ref, zv_ref, lng_ref, wm_ref, bias_ref, o_ref, *v_refs, heads):
    vpre = zv_ref[...].astype(F32)
    xc = vpre - jnp.mean(vpre, axis=-1, keepdims=True)
    v = xc * lax.rsqrt(jnp.mean(xc * xc, axis=-1, keepdims=True) + EPS) * lng_ref[...]
    if v_refs:
        v_refs[0][...] = v
    vb = v.astype(BF16)
    hd = vb.shape[1] // heads
    bias = bias_ref[...]
    for h in range(heads):
        mixed = _dot(wm_ref[h], vb[:, h * hd:(h + 1) * hd]) + bias[:, h:h + 1]
        o_ref[:, h * hd:(h + 1) * hd] = (zu_ref[:, h * hd:(h + 1) * hd].astype(F32) * mixed).astype(o_ref.dtype)


def _sg_mix(z, ln_g, wm, bias, *, emit_v):
    m = z.shape[0]
    width = z.shape[1] // 2
    heads, ln = wm.shape[0], wm.shape[1]
    out_shape = [jax.ShapeDtypeStruct((m, width), BF16)]
    out_specs = [pl.BlockSpec((ln, width), lambda c: (c, 0))]
    if emit_v:
        out_shape.append(jax.ShapeDtypeStruct((m, width), F32))
        out_specs.append(pl.BlockSpec((ln, width), lambda c: (c, 0)))
    res = pl.pallas_call(
        functools.partial(_sg_mix_kernel, heads=heads),
        grid=(m // ln,),
        in_specs=[
            pl.BlockSpec((ln, width), lambda c: (c, 0)),
            pl.BlockSpec((ln, width), lambda c: (c, 1)),
            pl.BlockSpec((1, width), lambda c: (0, 0)),
            pl.BlockSpec((heads, ln, ln), lambda c: (0, 0, 0)),
            pl.BlockSpec((ln, heads), lambda c: (0, 0)),
        ],
        out_specs=out_specs,
        out_shape=out_shape,
        compiler_params=_cparams(1),
        name="sg_mix",
    )(z, z, ln_g, wm, bias)
    return (res[0], res[1]) if emit_v else (res[0], None)


def _gla_kernel(q_ref, k_ref, v_ref, r_ref, glr_ref, wg_ref, bg_ref, ng_ref, s0_ref, o_ref, sout_ref, s_scr,
                *, blk, nsub, nstep, scale):
    n = pl.program_id(2)
    rows = blk * nsub
    assert nsub in (1, 2)

    @pl.when(n == 0)
    def _():
        s_scr[...] = s0_ref[...]

    pre = _dot(glr_ref[...], wg_ref[...]) + bg_ref[...]
    la = (jnp.minimum(pre, 0.0) - jnp.log1p(jnp.exp(-jnp.abs(pre)))) * (1.0 / GLA_TAU)
    ri = lax.broadcasted_iota(jnp.int32, (rows, rows), 0)
    ci = lax.broadcasted_iota(jnp.int32, (rows, rows), 1)
    causal = ri >= ci
    if nsub == 2:
        cross = (ri >= blk) & (ci < blk)
        causal = causal & jnp.logical_not(cross)
    tri = jnp.where(causal, 1.0, 0.0).astype(BF16)
    la1 = la.astype(BF16)
    rem = la - la1.astype(F32)
    la2 = rem.astype(BF16)
    la3 = (rem - la2.astype(F32)).astype(BF16)
    g = _dot(tri, la1) + _dot(tri, la2) + _dot(tri, la3)
    g_a = g[blk - 1:blk, :]
    q = q_ref[...].astype(F32) * scale
    k = k_ref[...].astype(F32)
    q_t = q * jnp.exp(g)
    k_t = (k * jnp.exp(-g)).astype(BF16)
    if nsub == 1:
        k_end = k * jnp.exp(g_a - g)
        q_inter, k_upd, log_decay = q_t, k_end, g_a
    else:
        g_b = g[rows - 1:rows, :]
        in_b = lax.broadcasted_iota(jnp.int32, (rows, 1), 0) >= blk
        k_end = k * jnp.exp(jnp.where(in_b, g_b, g_a) - g)
        q_inter = jnp.where(in_b, q_t * jnp.exp(g_a), q_t)
        k_upd = jnp.where(in_b, k_end, k_end * jnp.exp(g_b))
        log_decay = g_a + g_b
    q_t = q_t.astype(BF16)
    nt = (((1,), (1,)), ((), ()))
    att = jnp.where(causal, lax.dot_general(q_t, k_t, nt, preferred_element_type=F32), 0.0)
    if nsub == 2:
        att = jnp.where(cross, lax.dot_general(q_t, k_end.astype(BF16), nt, preferred_element_type=F32), att)
    v = v_ref[...]
    s = s_scr[...]
    o = _dot(att.astype(BF16), v) + _dot(q_inter.astype(BF16), s.astype(BF16))
    decay = jnp.transpose(jnp.broadcast_to(jnp.exp(log_decay), (LANES, g.shape[1])))[:, 0:1]
    s_new = decay * s + lax.dot_general(k_upd.astype(BF16), v, (((0,), (0,)), ((), ())),
                                        preferred_element_type=F32)
    s_scr[...] = s_new

    @pl.when(n == nstep - 1)
    def _():
        sout_ref[...] = s_new

    r = r_ref[...].astype(F32)
    o_ref[...] = (_rms(o) * ng_ref[...] * (r * _sigmoid(r))).astype(o_ref.dtype)


def _gla_core(proj, glr, w_gate, b_gate, norm_g, s0, *, seq, heads):
    m = proj.shape[0]
    nbatch = m // seq
    dk, dv = s0.shape[2], s0.shape[3]
    blk = GLA_BLOCK if seq % GLA_BLOCK == 0 else seq
    nsub = 2 if (seq // blk) % 2 == 0 else 1
    nstep = seq // (blk * nsub)
    rows = blk * nsub
    kq = heads
    row = lambda b, h, n: b * nstep + n
    out, s_new = pl.pallas_call(
        functools.partial(_gla_kernel, blk=blk, nsub=nsub, nstep=nstep, scale=float(dk) ** -0.5),
        grid=(nbatch, heads, nstep),
        in_specs=[
            pl.BlockSpec((rows, dk), lambda b, h, n: (row(b, h, n), h)),
            pl.BlockSpec((rows, dk), lambda b, h, n: (row(b, h, n), kq + h)),
            pl.BlockSpec((rows, dv), lambda b, h, n: (row(b, h, n), (2 * heads * dk) // dv + h)),
            pl.BlockSpec((rows, dv), lambda b, h, n: (row(b, h, n), (2 * heads * dk) // dv + heads + h)),
            pl.BlockSpec((rows, LANES), lambda b, h, n: (row(b, h, n), 0)),
            pl.BlockSpec((LANES, dk), lambda b, h, n: (0, h)),
            pl.BlockSpec((1, dk), lambda b, h, n: (0, h)),
            pl.BlockSpec((1, dv), lambda b, h, n: (0, 0)),
            pl.BlockSpec((None, None, dk, dv), lambda b, h, n: (b, h, 0, 0)),
        ],
        out_specs=[
            pl.BlockSpec((rows, dv), lambda b, h, n: (row(b, h, n), h)),
            pl.BlockSpec((None, None, dk, dv), lambda b, h, n: (b, h, 0, 0)),
        ],
        out_shape=[jax.ShapeDtypeStruct((m, heads * dv), BF16), jax.ShapeDtypeStruct(s0.shape, F32)],
        scratch_shapes=[pltpu.VMEM((dk, dv), F32)],
        compiler_params=_cparams(3),
        name="gla_core",
    )(proj, proj, proj, proj, glr, w_gate, b_gate, norm_g, s0)
    return out, s_new


def _qattn_kernel(h_ref, wq_ref, k_ref, v_ref, o_ref, *, ts, nb, scale):
    def attend(q, s):
        kb = k_ref[s].astype(BF16)
        vb = v_ref[s].astype(BF16)
        sc = lax.dot_general(q, kb, (((1,), (1,)), ((), ())), preferred_element_type=F32) * scale
        p = jnp.exp(sc - jnp.max(sc, axis=-1, keepdims=True))
        p = p / jnp.sum(p, axis=-1, keepdims=True)
        return _dot(p.astype(BF16), vb).astype(o_ref.dtype)

    if nb == 1:
        rc = min(ts, ATTN_ROW_CHUNK)
        for c in range(ts // rc):
            rows = slice(c * rc, (c + 1) * rc)
            o_ref[rows, :] = attend(_dot(h_ref[rows, :], wq_ref[...]).astype(BF16), 0)
    else:
        q = _dot(h_ref[...], wq_ref[...]).astype(BF16)
        for s in range(nb):
            rows = slice(s * ts, (s + 1) * ts)
            o_ref[rows, :] = attend(q[rows, :], s)


def _q_attn(h, wq, mk, mv, layer, *, seq, heads, tm):
    m, k = h.shape
    hd = wq.shape[2] // heads
    mem_len = mk.shape[2]
    ts = min(tm, seq)
    nb = tm // ts
    tiles_per_seq = seq // ts
    kv_map = lambda i, hh: (layer, i // tiles_per_seq, 0, hh)
    return pl.pallas_call(
        functools.partial(_qattn_kernel, ts=ts, nb=nb, scale=float(hd) ** -0.5),
        grid=(m // tm, heads),
        in_specs=[
            pl.BlockSpec((tm, k), lambda i, hh: (i, 0)),
            pl.BlockSpec((None, k, hd), lambda i, hh: (layer, 0, hh)),
            pl.BlockSpec((None, nb, mem_len, hd), kv_map),
            pl.BlockSpec((None, nb, mem_len, hd), kv_map),
        ],
        out_specs=pl.BlockSpec((tm, hd), lambda i, hh: (i, hh)),
        out_shape=jax.ShapeDtypeStruct((m, heads * hd), BF16),
        compiler_params=_cparams(2),
        name="q_attn",
    )(h, wq, mk, mv)


def _mem_kv_kernel(mem_ref, g_ref, wk_ref, wv_ref, k_ref, v_ref, hn_scr):
    @pl.when(pl.program_id(2) == 0)
    def _():
        hn_scr[...] = (_rms(mem_ref[...]) * g_ref[...]).astype(BF16)

    hn = hn_scr[...]
    k_ref[...] = _dot(hn, wk_ref[...])
    v_ref[...] = _dot(hn, wv_ref[...])


def _mem_kv(mem, g, w_kv, *, tm, tn):
    m, d = mem.shape
    depth = w_kv.shape[0]
    n = w_kv.shape[2] // 2
    nj = n // tn
    return pl.pallas_call(
        _mem_kv_kernel,
        grid=(depth, m // tm, nj),
        in_specs=[
            pl.BlockSpec((tm, d), lambda l, i, j: (i, 0)),
            pl.BlockSpec((None, 1, d), lambda l, i, j: (l, 0, 0)),
            pl.BlockSpec((None, d, tn), lambda l, i, j: (l, 0, j)),
            pl.BlockSpec((None, d, tn), lambda l, i, j: (l, 0, j + nj)),
        ],
        out_specs=[pl.BlockSpec((None, tm, tn), lambda l, i, j: (l, i, j)),
                   pl.BlockSpec((None, tm, tn), lambda l, i, j: (l, i, j))],
        out_shape=[jax.ShapeDtypeStruct((depth, m, n), F32), jax.ShapeDtypeStruct((depth, m, n), F32)],
        scratch_shapes=[pltpu.VMEM((tm, d), BF16)],
        compiler_params=_cparams(3),
        name="mem_kv",
    )(mem, g, w_kv, w_kv)


OUT_COL_CHUNK = 1024
OUT_EPI_ROWS = 64
OUT_ROW_GROUP = 128


def _accumulate(lhs_ref, w_ref, acc_ref, first):
    lhs = lhs_ref[...]
    d = acc_ref.shape[1]
    tn = min(d, OUT_COL_CHUNK)
    for c in range(d // tn):
        cs = slice(c * tn, (c + 1) * tn)
        part = _dot(lhs, w_ref[:, cs])
        acc_ref[:, cs] = part if first else acc_ref[:, cs] + part


def _res_norm_tail(acc_ref, x_ref, gpost_ref, gnext_ref, xo_ref, hn_ref, *, unrolled=False):
    gpost = gpost_ref[...]
    gnext = gnext_ref[...]
    rows = min(acc_ref.shape[0], OUT_EPI_ROWS)

    def chunk(sl):
        xn = x_ref[sl, :] + _rms(acc_ref[sl, :]) * gpost
        if xo_ref is not None:
            xo_ref[sl, :] = xn
        hn_ref[sl, :] = (_rms(xn) * gnext).astype(hn_ref.dtype)

    if unrolled:
        for c in range(acc_ref.shape[0] // rows):
            chunk(slice(c * rows, (c + 1) * rows))
    else:
        def body(c, carry):
            chunk(pl.ds(pl.multiple_of(c * rows, rows), rows))
            return carry

        lax.fori_loop(0, acc_ref.shape[0] // rows, body, 0)


def _outproj_ktiled_kernel(lhs_ref, w_ref, x_hbm, gpost_ref, gnext_ref, *rest, nk, emit_x):
    if emit_x:
        xo_ref, hn_ref, x_buf, x_sem = rest
        acc_ref = xo_ref
    else:
        hn_ref, acc_ref, x_buf, x_sem = rest
        xo_ref = None
    tm = x_buf.shape[0]
    row0 = pl.multiple_of(pl.program_id(0) * tm, tm)
    x_copy = pltpu.make_async_copy(x_hbm.at[pl.ds(row0, tm), :], x_buf, x_sem)
    kk = pl.program_id(1)

    assert nk >= 2

    @pl.when(kk == 0)
    def _():
        x_copy.start()
        _accumulate(lhs_ref, w_ref, acc_ref, True)

    pl.when((kk > 0) & (kk < nk - 1))(functools.partial(_accumulate, lhs_ref, w_ref, acc_ref, False))

    @pl.when(kk == nk - 1)
    def _():
        x_copy.wait()
        group = min(tm, OUT_ROW_GROUP)
        for r in range(tm // group):
            rows = slice(r * group, (r + 1) * group)
            _accumulate(lhs_ref.at[rows], w_ref, acc_ref.at[rows], False)
            _res_norm_tail(acc_ref.at[rows], x_buf.at[rows], gpost_ref, gnext_ref,
                           None if xo_ref is None else xo_ref.at[rows], hn_ref.at[rows], unrolled=True)


def _outproj_resident_kernel(lhs_ref, w_ref, x_ref, gpost_ref, gnext_ref, xo_ref, hn_ref):
    tm = xo_ref.shape[0]
    group = min(tm, OUT_ROW_GROUP)
    for r in range(tm // group):
        rows = slice(r * group, (r + 1) * group)
        _accumulate(lhs_ref.at[rows], w_ref, xo_ref.at[rows], True)
        _res_norm_tail(xo_ref.at[rows], x_ref.at[rows], gpost_ref, gnext_ref, xo_ref.at[rows], hn_ref.at[rows],
                       unrolled=True)


def _outproj(lhs, w, layer, x, g_post, g_next, *, hn_dtype, emit_x, tm, tk):
    m, kdim = lhs.shape
    d = w.shape[2]
    out_shape = [jax.ShapeDtypeStruct((m, d), hn_dtype)]
    if emit_x:
        out_shape.insert(0, jax.ShapeDtypeStruct((m, d), F32))
    if tk is None:
        assert emit_x
        row_map = lambda i: (i, 0)
        fix_map = lambda i: (0, 0)
        return pl.pallas_call(
            _outproj_resident_kernel,
            grid=(m // tm,),
            in_specs=[
                pl.BlockSpec((tm, kdim), row_map),
                pl.BlockSpec((None, kdim, d), lambda i: (layer, 0, 0), pipeline_mode=pl.Buffered(1)),
                pl.BlockSpec((tm, d), row_map),
                pl.BlockSpec((1, d), fix_map),
                pl.BlockSpec((1, d), fix_map),
            ],
            out_specs=[pl.BlockSpec((tm, d), row_map), pl.BlockSpec((tm, d), row_map)],
            out_shape=out_shape,
            compiler_params=_cparams(1, V7X_VMEM_LIMIT_OUTPROJ_BYTES),
            name="outproj_resident",
        )(lhs, w, x, g_post, g_next)
    nk = kdim // tk
    row_map = lambda i, kk: (i, 0)
    fix_map = lambda i, kk: (0, 0)
    return pl.pallas_call(
        functools.partial(_outproj_ktiled_kernel, nk=nk, emit_x=emit_x),
        grid=(m // tm, nk),
        in_specs=[
            pl.BlockSpec((tm, tk), lambda i, kk: (i, kk)),
            pl.BlockSpec((None, tk, d), lambda i, kk: (layer, kk, 0)),
            pl.BlockSpec(memory_space=pl.ANY),
            pl.BlockSpec((1, d), fix_map),
            pl.BlockSpec((1, d), fix_map),
        ],
        out_specs=[pl.BlockSpec((tm, d), row_map)] * len(out_shape),
        out_shape=out_shape,
        scratch_shapes=([] if emit_x else [pltpu.VMEM((tm, d), F32)])
        + [pltpu.VMEM((tm, d), F32), pltpu.SemaphoreType.DMA(())],
        compiler_params=_cparams(2, V7X_VMEM_LIMIT_OUTPROJ_BYTES),
        name="outproj_ktiled",
    )(lhs, w, x, g_post, g_next)


def _pad_last(a, n):
    return jnp.pad(a, [(0, 0)] * (a.ndim - 1) + [(0, n - a.shape[-1])])


def _tiles(m):
    big = m >= 1024
    return dict(
        norm=256 if big else m,
        proj=1024 if big else m,
        sc=512 if big else m,
        attn=1024 if big else m,
        out=512 if big else m,
        out_resident=256 if big else m,
    )


def _sg_weights(w_s, b_s, seq):
    heads, ln, _ = w_s.shape
    pos = jnp.arange(ln) // CHUNK
    w = jnp.where(pos[None, :] <= pos[:, None], w_s, jnp.zeros_like(w_s))
    if seq % ln == 0:
        return w.astype(BF16), b_s.T
    assert ln % seq == 0
    reps = ln // seq
    blockdiag = jnp.kron(jnp.eye(reps, dtype=w.dtype), jnp.ones((seq, seq), w.dtype))
    wm = jnp.tile(w[:, :seq, :seq], (1, reps, reps)) * blockdiag
    return wm.astype(BF16), jnp.tile(b_s[:, :seq].T, (reps, 1))


def kernel(x_prompt, x_sample, mem_prompt, state_gla, cache_conv_c, cache_ffn_conv, cache_mem_k, cache_mem_v,
           norm_g, mem_norm_g, final_g,
           sg_w_in, sg_ln_g, sg_w_s, sg_b_s, sg_w_out,
           gla_w_in, gla_w_gate, gla_b_gate, gla_norm_g, gla_w_out,
           sc_w_in, sc_conv, sc_w_out,
           mem_w_q, mem_w_kv, mem_w_o,
           ffn_w_up, ffn_conv, ffn_w_down):
    depth = norm_g.shape[0]
    d_model = x_prompt.shape[-1]
    d_ff = ffn_conv.shape[-1]
    d_ffp = -(-d_ff // FFN_PAD) * FFN_PAD
    gla_heads, gla_dk, gla_dv = state_gla.shape[2], state_gla.shape[3], state_gla.shape[4]
    gla_main = 2 * gla_heads * (gla_dk + gla_dv)
    mem_heads = cache_mem_k.shape[3]
    mem_len = cache_mem_k.shape[2]
    row = lambda a: a.reshape(1, -1)

    def cast(w, **kw):
        kw.setdefault("tr", min(w.shape[1], CAST_TILE))
        kw.setdefault("tc", min(w.shape[2], CAST_TILE))
        return _cast_weights(w, **kw)

    w_sg_in, w_sg_out = cast(sg_w_in), cast(sg_w_out)
    w_gla_in = cast(gla_w_in, group_cols=gla_main)
    w_gla_lr = cast(gla_w_in, col_offset=gla_main, group_cols=gla_w_in.shape[2] - gla_main, out_group_cols=LANES,
                    tr=d_model, tc=LANES)
    w_gla_gate = [jnp.pad(gla_w_gate[b].astype(BF16), ((0, LANES - gla_w_gate.shape[1]), (0, 0)))
                  for b in range(gla_w_gate.shape[0])]
    w_gla_out = cast(gla_w_out)
    w_sc_in, w_sc_out = cast(sc_w_in), cast(sc_w_out)
    w_q, w_kv, w_o = cast(mem_w_q), cast(mem_w_kv), cast(mem_w_o)
    w_up = cast(ffn_w_up, groups=2, group_cols=d_ff, out_group_cols=d_ffp, tr=d_model, tc=FFN_CAST_TILE)
    w_down = cast(ffn_w_down, out_rows=d_ffp, tr=FFN_CAST_TILE, tc=d_model)
    ffn_cw = [_pad_last(ffn_conv[l], d_ffp) for l in range(depth)]

    def run(x, seq, mk, mv, s_gla, buf_c, buf_f, emit_v):
        m = x.shape[0]
        t = _tiles(m)
        sg_v, gla_out, c_out, f_out = [], [], [], []
        ia = ib = ic = 0
        hn = _rms_cast(x, row(norm_g[0, 0]), tm=t["norm"])
        for l in range(depth):
            kind = l % N_MIXERS
            if kind == 0:
                z = _proj(hn, w_sg_in, ia, act="gelu", tm=t["proj"], tn=PLAIN_TN)
                wm, bias = _sg_weights(sg_w_s[ia], sg_b_s[ia], seq)
                lhs, v_rows = _sg_mix(z, row(sg_ln_g[ia]), wm, bias, emit_v=emit_v)
                sg_v.append(v_rows)
                w_out, i_out = w_sg_out, ia
                ia += 1
            elif kind == 1:
                proj = _proj(hn, w_gla_in, ib, tm=t["proj"], tn=PLAIN_TN)
                glr = _proj(hn, w_gla_lr, ib, tm=t["proj"], tn=LANES)
                lhs, s_new = _gla_core(proj, glr, w_gla_gate[ib], row(gla_b_gate[ib]), row(gla_norm_g[ib]),
                                       s_gla[ib], seq=seq, heads=gla_heads)
                gla_out.append(s_new)
                w_out, i_out = w_gla_out, ib
                ib += 1
            else:
                lhs, cb = _proj_conv_gate(hn, w_sc_in, ic, sc_conv[ic], buf_c[ic], mode="sc", seq=seq,
                                          tm=t["sc"], tn=PROJ_TN)
                c_out.append(cb)
                w_out, i_out = w_sc_out, ic
                ic += 1
            x, hn = _outproj(lhs, w_out, i_out, x, row(norm_g[l, 1]), row(norm_g[l, 2]), hn_dtype=BF16, emit_x=True,
                             tm=t["out_resident"], tk=None)
            att = _q_attn(hn, w_q, mk, mv, l, seq=seq, heads=mem_heads, tm=t["attn"])
            x, hn = _outproj(att, w_o, l, x, row(norm_g[l, 3]), row(norm_g[l, 4]), hn_dtype=BF16, emit_x=True,
                             tm=t["out_resident"], tk=None)
            hid, fb = _proj_conv_gate(hn, w_up, l, ffn_cw[l], _pad_last(buf_f[l], d_ffp), mode="ffn", seq=seq,
                                      tm=t["proj"], tn=PROJ_TN)
            f_out.append(fb[:, :, :d_ff])
            if l < depth - 1:
                x, hn = _outproj(hid, w_down, l, x, row(norm_g[l, 5]), row(norm_g[l + 1, 0]), hn_dtype=BF16,
                                 emit_x=True, tm=t["out"], tk=OUT_TK)
            else:
                hn, = _outproj(hid, w_down, l, x, row(norm_g[l, 5]), row(final_g), hn_dtype=F32,
                               emit_x=False, tm=t["out"], tk=OUT_TK)
        return hn, sg_v, gla_out, c_out, f_out

    bp, sp, _ = x_prompt.shape
    bs, ss, _ = x_sample.shape

    mem2d = mem_prompt.reshape(bp * mem_len, d_model)
    mem_k_p, mem_v_p = _mem_kv(mem2d, mem_norm_g.reshape(depth, 1, d_model), w_kv, tm=MEM_KV_TILE, tn=MEM_KV_TILE)
    mem_k_p = mem_k_p.reshape(depth, bp, mem_len, d_model)
    mem_v_p = mem_v_p.reshape(depth, bp, mem_len, d_model)
    y_p, _, gla_p, conv_p, ffn_p = run(
        x_prompt.reshape(bp * sp, d_model), sp, mem_k_p, mem_v_p,
        jnp.zeros((state_gla.shape[0], bp) + state_gla.shape[2:], F32),
        jnp.zeros((cache_conv_c.shape[0], bp, 2, d_model), F32),
        jnp.zeros((depth, bp, 2, d_ff), F32), emit_v=False)

    y_s, sgv_s, gla_s, conv_s, ffn_s = run(
        x_sample.reshape(bs * ss, d_model), ss,
        cache_mem_k.reshape(depth, bs, mem_len, -1), cache_mem_v.reshape(depth, bs, mem_len, -1),
        state_gla, cache_conv_c, cache_ffn_conv, emit_v=True)

    kv_shape = (depth, bp, mem_len, mem_heads, d_model // mem_heads)
    return (y_p.reshape(bp, sp, d_model), y_s.reshape(bs, ss, d_model),
            jnp.stack(gla_p), jnp.stack(conv_p), jnp.stack(ffn_p),
            mem_k_p.reshape(kv_shape), mem_v_p.reshape(kv_shape),
            jnp.stack([v.reshape(bs, ss, -1) for v in sgv_s]), jnp.stack(gla_s), jnp.stack(conv_s), jnp.stack(ffn_s))
```

```python
import functools

import jax
import jax.numpy as jnp
from jax import lax
from jax.experimental import pallas as pl
from jax.experimental.pallas import tpu as pltpu

F32 = jnp.float32
BF16 = jnp.bfloat16

EPS = 1e-6
CHUNK = 64
GLA_TAU = 16.0
GLA_BLOCK = 64
N_MIXERS = 3

V7X_VMEM_LIMIT_BYTES = 56 * 1024 * 1024
V7X_VMEM_LIMIT_OUTPROJ_BYTES = 60 * 1024 * 1024
LANES = 128
FFN_PAD = 1024
PCG_ROW_CHUNK = 256
ATTN_ROW_CHUNK = 512
PROJ_TN = 512
PLAIN_TN = 1024
OUT_TK = 1024
MEM_KV_TILE = 512
CAST_TILE = 1024
FFN_CAST_TILE = 256


def _cparams(n_axes, vmem_limit=V7X_VMEM_LIMIT_BYTES):
    return pltpu.CompilerParams(dimension_semantics=("arbitrary",) * n_axes, vmem_limit_bytes=vmem_limit)


def _dot(a, b):
    return jnp.dot(a, b, preferred_element_type=F32)


def _rms(x):
    return x * lax.rsqrt(jnp.mean(x * x, axis=-1, keepdims=True) + EPS)


def _sigmoid(x):
    return 1.0 / (1.0 + jnp.exp(-x))


def _rms_cast_kernel(x_ref, g_ref, o_ref):
    o_ref[...] = (_rms(x_ref[...]) * g_ref[...]).astype(o_ref.dtype)


def _rms_cast(x, g, *, tm):
    m, d = x.shape
    return pl.pallas_call(
        _rms_cast_kernel,
        grid=(m // tm,),
        in_specs=[pl.BlockSpec((tm, d), lambda i: (i, 0)), pl.BlockSpec((1, d), lambda i: (0, 0))],
        out_specs=pl.BlockSpec((tm, d), lambda i: (i, 0)),
        out_shape=jax.ShapeDtypeStruct((m, d), BF16),
        compiler_params=_cparams(1),
        name="rms_cast",
    )(x, g)


def _proj_kernel(h_ref, w_ref, o_ref, *, act):
    tm = h_ref.shape[0]
    rc = min(tm, PCG_ROW_CHUNK)
    for c in range(tm // rc):
        rows = slice(c * rc, (c + 1) * rc)
        acc = _dot(h_ref[rows, :], w_ref[...])
        if act == "gelu":
            acc = 0.5 * acc * (1.0 + jnp.tanh(0.7978845608028654 * (acc + 0.044715 * (acc * acc * acc))))
        o_ref[rows, :] = acc.astype(o_ref.dtype)


def _proj(h, w, layer, *, act=None, tm, tn):
    m, k = h.shape
    n = w.shape[2]
    return pl.pallas_call(
        functools.partial(_proj_kernel, act=act),
        grid=(n // tn, m // tm),
        in_specs=[pl.BlockSpec((tm, k), lambda j, i: (i, 0)), pl.BlockSpec((None, k, tn), lambda j, i: (layer, 0, j))],
        out_specs=pl.BlockSpec((tm, tn), lambda j, i: (i, j)),
        out_shape=jax.ShapeDtypeStruct((m, n), BF16),
        compiler_params=_cparams(2),
        name="proj_" + (act or "plain"),
    )(h, w)


def _cast_kernel(src_ref, dst_ref, *, src_rows, group_cols, masked):
    v = src_ref[...]
    if masked:
        tr, tc = v.shape
        rows = lax.broadcasted_iota(jnp.int32, (tr, 1), 0) + pl.program_id(1) * tr
        cols = lax.broadcasted_iota(jnp.int32, (1, tc), 1) + pl.program_id(3) * tc
        v = jnp.where((rows < src_rows) & (cols < group_cols), v, 0.0)
    dst_ref[...] = v.astype(dst_ref.dtype)


def _cast_weights(src, *, out_rows=None, out_group_cols=None, groups=1, group_cols=None, col_offset=0, tr, tc):
    nl, src_rows, src_cols = src.shape
    out_rows = out_rows or src_rows
    group_cols = group_cols or src_cols
    out_group_cols = out_group_cols or group_cols
    assert out_rows % tr == 0 and out_group_cols % tc == 0 and col_offset % tc == 0
    assert groups == 1 or group_cols % tc == 0
    masked = out_rows != src_rows or out_group_cols != group_cols
    last_rb = (src_rows - 1) // tr
    last_cb = (src_cols - 1) // tc
    ncb = out_group_cols // tc
    src_map = lambda l, r, g, c: (l, jnp.minimum(r, last_rb),
                                  jnp.minimum((col_offset + g * group_cols) // tc + c, last_cb))
    return pl.pallas_call(
        functools.partial(_cast_kernel, src_rows=src_rows, group_cols=group_cols, masked=masked),
        grid=(nl, out_rows // tr, groups, ncb),
        in_specs=[pl.BlockSpec((None, tr, tc), src_map)],
        out_specs=pl.BlockSpec((None, tr, tc), lambda l, r, g, c: (l, r, g * ncb + c)),
        out_shape=jax.ShapeDtypeStruct((nl, out_rows, groups * out_group_cols), BF16),
        compiler_params=_cparams(4),
        name="cast_weights",
    )(src)


def _pcg_kernel(*refs, mode, tm, seq, nseg):
    if mode == "ffn":
        h_ref, wz_ref, wg_ref, cw_ref, pre_ref, o_ref, cache_ref, scr = refs
    else:
        h_ref, wg_ref, wz_ref, wx_ref, cw_ref, pre_ref, o_ref, cache_ref, scr = refs
    i = pl.program_id(1)
    cw = cw_ref[...]

    if nseg == 1:
        first = (i % (seq // tm)) == 0

        @pl.when(first)
        def _():
            scr[6:8, :] = pre_ref[0]

        @pl.when(jnp.logical_not(first))
        def _():
            scr[6:8, :] = scr[tm + 6:tm + 8, :]
    else:
        scr[6:8, :] = pre_ref[0]

    rc = min(tm, PCG_ROW_CHUNK)
    for c in range(tm // rc):
        r0 = c * rc
        hv = h_ref[r0:r0 + rc, :]
        zc = _dot(hv, wz_ref[...])
        if mode == "sc":
            zc = zc * _dot(hv, wx_ref[...])
        gate = _dot(hv, wg_ref[...])
        scr[8 + r0:8 + r0 + rc, :] = zc
        sh1 = scr[7 + r0:7 + r0 + rc, :]
        sh2 = scr[6 + r0:6 + r0 + rc, :]
        if nseg > 1:
            row = lax.broadcasted_iota(jnp.int32, (rc, 1), 0) + r0
            for s in range(nseg):
                p = pre_ref[s]
                sh1 = jnp.where(row == s * seq, p[1:2, :], sh1)
                sh2 = jnp.where(row == s * seq, p[0:1, :], sh2)
                sh2 = jnp.where(row == s * seq + 1, p[1:2, :], sh2)
        y = cw[0:1, :] * sh2 + cw[1:2, :] * sh1 + cw[2:3, :] * zc
        if mode == "ffn":
            out = (y * _sigmoid(y)) * gate
        else:
            out = gate * y
        o_ref[r0:r0 + rc, :] = out.astype(o_ref.dtype)
    if nseg == 1:
        cache_ref[0] = scr[tm + 6:tm + 8, :]
    else:
        for s in range(nseg):
            cache_ref[s] = scr[8 + (s + 1) * seq - 2:8 + (s + 1) * seq, :]


def _proj_conv_gate(h, w, layer, conv_w, prefix, *, mode, seq, tm, tn):
    m, k = h.shape
    width = conv_w.shape[1]
    nj = width // tn
    assert seq >= 2 and (seq % tm == 0 or tm % seq == 0)
    nseg = max(1, tm // seq)
    if nseg == 1:
        tiles_per_seq = seq // tm
        pre_map = lambda j, i: (i // tiles_per_seq, 0, j)
    else:
        pre_map = lambda j, i: (i, 0, j)
    n_groups = 2 if mode == "ffn" else 3
    w_specs = [pl.BlockSpec((None, k, tn), functools.partial(lambda j, i, g: (layer, 0, j + g * nj), g=g))
               for g in range(n_groups)]
    nbatch = m // seq
    out, cache = pl.pallas_call(
        functools.partial(_pcg_kernel, mode=mode, tm=tm, seq=seq, nseg=nseg),
        grid=(nj, m // tm),
        in_specs=[pl.BlockSpec((tm, k), lambda j, i: (i, 0))] + w_specs + [
            pl.BlockSpec((3, tn), lambda j, i: (0, j)),
            pl.BlockSpec((nseg, 2, tn), pre_map),
        ],
        out_specs=[pl.BlockSpec((tm, tn), lambda j, i: (i, j)), pl.BlockSpec((nseg, 2, tn), pre_map)],
        out_shape=[jax.ShapeDtypeStruct((m, width), BF16), jax.ShapeDtypeStruct((nbatch, 2, width), F32)],
        scratch_shapes=[pltpu.VMEM((tm + 8, tn), F32)],
        compiler_params=_cparams(2),
        name="proj_conv_gate_" + mode,
    )(h, *([w] * n_groups), conv_w, prefix)
    return out, cache


def _sg_mix_kernel(zu_ref, zv_ref, lng_ref, wm_ref, bias_ref, o_ref, *v_refs, heads):
    vpre = zv_ref[...].astype(F32)
    xc = vpre - jnp.mean(vpre, axis=-1, keepdims=True)
    v = xc * lax.rsqrt(jnp.mean(xc * xc, axis=-1, keepdims=True) + EPS) * lng_ref[...]
    if v_refs:
        v_refs[0][...] = v
    vb = v.astype(BF16)
    hd = vb.shape[1] // heads
    bias = bias_ref[...]
    for h in range(heads):
        mixed = _dot(wm_ref[h], vb[:, h * hd:(h + 1) * hd]) + bias[:, h:h + 1]
        o_ref[:, h * hd:(h + 1) * hd] = (zu_ref[:, h * hd:(h + 1) * hd].astype(F32) * mixed).astype(o_ref.dtype)


def _sg_mix(z, ln_g, wm, bias, *, emit_v):
    m = z.shape[0]
    width = z.shape[1] // 2
    heads, ln = wm.shape[0], wm.shape[1]
    out_shape = [jax.ShapeDtypeStruct((m, width), BF16)]
    out_specs = [pl.BlockSpec((ln, width), lambda c: (c, 0))]
    if emit_v:
        out_shape.append(jax.ShapeDtypeStruct((m, width), F32))
        out_specs.append(pl.BlockSpec((ln, width), lambda c: (c, 0)))
    res = pl.pallas_call(
        functools.partial(_sg_mix_kernel, heads=heads),
        grid=(m // ln,),
        in_specs=[
            pl.BlockSpec((ln, width), lambda c: (c, 0)),
            pl.BlockSpec((ln, width), lambda c: (c, 1)),
            pl.BlockSpec((1, width), lambda c: (0, 0)),
            pl.BlockSpec((heads, ln, ln), lambda c: (0, 0, 0)),
            pl.BlockSpec((ln, heads), lambda c: (0, 0)),
        ],
        out_specs=out_specs,
        out_shape=out_shape,
        compiler_params=_cparams(1),
        name="sg_mix",
    )(z, z, ln_g, wm, bias)
    return (res[0], res[1]) if emit_v else (res[0], None)


def _gla_kernel(q_ref, k_ref, v_ref, r_ref, glr_ref, wg_ref, bg_ref, ng_ref, s0_ref, o_ref, sout_ref, s_scr,
                *, blk, nsub, nstep, scale):
    n = pl.program_id(2)
    rows = blk * nsub
    assert nsub in (1, 2)

    @pl.when(n == 0)
    def _():
        s_scr[...] = s0_ref[...]

    pre = _dot(glr_ref[...], wg_ref[...]) + bg_ref[...]
    la = (jnp.minimum(pre, 0.0) - jnp.log1p(jnp.exp(-jnp.abs(pre)))) * (1.0 / GLA_TAU)
    ri = lax.broadcasted_iota(jnp.int32, (rows, rows), 0)
    ci = lax.broadcasted_iota(jnp.int32, (rows, rows), 1)
    causal = ri >= ci
    if nsub == 2:
        cross = (ri >= blk) & (ci < blk)
        causal = causal & jnp.logical_not(cross)
    tri = jnp.where(causal, 1.0, 0.0).astype(BF16)
    la1 = la.astype(BF16)
    rem = la - la1.astype(F32)
    la2 = rem.astype(BF16)
    la3 = (rem - la2.astype(F32)).astype(BF16)
    g = _dot(tri, la1) + _dot(tri, la2) + _dot(tri, la3)
    g_a = g[blk - 1:blk, :]
    q = q_ref[...].astype(F32) * scale
    k = k_ref[...].astype(F32)
    q_t = q * jnp.exp(g)
    k_t = (k * jnp.exp(-g)).astype(BF16)
    if nsub == 1:
        k_end = k * jnp.exp(g_a - g)
        q_inter, k_upd, log_decay = q_t, k_end, g_a
    else:
        g_b = g[rows - 1:rows, :]
        in_b = lax.broadcasted_iota(jnp.int32, (rows, 1), 0) >= blk
        k_end = k * jnp.exp(jnp.where(in_b, g_b, g_a) - g)
        q_inter = jnp.where(in_b, q_t * jnp.exp(g_a), q_t)
        k_upd = jnp.where(in_b, k_end, k_end * jnp.exp(g_b))
        log_decay = g_a + g_b
    q_t = q_t.astype(BF16)
    nt = (((1,), (1,)), ((), ()))
    att = jnp.where(causal, lax.dot_general(q_t, k_t, nt, preferred_element_type=F32), 0.0)
    if nsub == 2:
        att = jnp.where(cross, lax.dot_general(q_t, k_end.astype(BF16), nt, preferred_element_type=F32), att)
    v = v_ref[...]
    s = s_scr[...]
    o = _dot(att.astype(BF16), v) + _dot(q_inter.astype(BF16), s.astype(BF16))
    decay = jnp.transpose(jnp.broadcast_to(jnp.exp(log_decay), (LANES, g.shape[1])))[:, 0:1]
    s_new = decay * s + lax.dot_general(k_upd.astype(BF16), v, (((0,), (0,)), ((), ())),
                                        preferred_element_type=F32)
    s_scr[...] = s_new

    @pl.when(n == nstep - 1)
    def _():
        sout_ref[...] = s_new

    r = r_ref[...].astype(F32)
    o_ref[...] = (_rms(o) * ng_ref[...] * (r * _sigmoid(r))).astype(o_ref.dtype)


def _gla_core(proj, glr, w_gate, b_gate, norm_g, s0, *, seq, heads):
    m = proj.shape[0]
    nbatch = m // seq
    dk, dv = s0.shape[2], s0.shape[3]
    blk = GLA_BLOCK if seq % GLA_BLOCK == 0 else seq
    nsub = 2 if (seq // blk) % 2 == 0 else 1
    nstep = seq // (blk * nsub)
    rows = blk * nsub
    kq = heads
    row = lambda b, h, n: b * nstep + n
    out, s_new = pl.pallas_call(
        functools.partial(_gla_kernel, blk=blk, nsub=nsub, nstep=nstep, scale=float(dk) ** -0.5),
        grid=(nbatch, heads, nstep),
        in_specs=[
            pl.BlockSpec((rows, dk), lambda b, h, n: (row(b, h, n), h)),
            pl.BlockSpec((rows, dk), lambda b, h, n: (row(b, h, n), kq + h)),
            pl.BlockSpec((rows, dv), lambda b, h, n: (row(b, h, n), (2 * heads * dk) // dv + h)),
            pl.BlockSpec((rows, dv), lambda b, h, n: (row(b, h, n), (2 * heads * dk) // dv + heads + h)),
            pl.BlockSpec((rows, LANES), lambda b, h, n: (row(b, h, n), 0)),
            pl.BlockSpec((LANES, dk), lambda b, h, n: (0, h)),
            pl.BlockSpec((1, dk), lambda b, h, n: (0, h)),
            pl.BlockSpec((1, dv), lambda b, h, n: (0, 0)),
            pl.BlockSpec((None, None, dk, dv), lambda b, h, n: (b, h, 0, 0)),
        ],
        out_specs=[
            pl.BlockSpec((rows, dv), lambda b, h, n: (row(b, h, n), h)),
            pl.BlockSpec((None, None, dk, dv), lambda b, h, n: (b, h, 0, 0)),
        ],
        out_shape=[jax.ShapeDtypeStruct((m, heads * dv), BF16), jax.ShapeDtypeStruct(s0.shape, F32)],
        scratch_shapes=[pltpu.VMEM((dk, dv), F32)],
        compiler_params=_cparams(3),
        name="gla_core",
    )(proj, proj, proj, proj, glr, w_gate, b_gate, norm_g, s0)
    return out, s_new


def _qattn_kernel(h_ref, wq_ref, k_ref, v_ref, o_ref, *, ts, nb, scale):
    def attend(q, s):
        kb = k_ref[s].astype(BF16)
        vb = v_ref[s].astype(BF16)
        sc = lax.dot_general(q, kb, (((1,), (1,)), ((), ())), preferred_element_type=F32) * scale
        p = jnp.exp(sc - jnp.max(sc, axis=-1, keepdims=True))
        p = p / jnp.sum(p, axis=-1, keepdims=True)
        return _dot(p.astype(BF16), vb).astype(o_ref.dtype)

    if nb == 1:
        rc = min(ts, ATTN_ROW_CHUNK)
        for c in range(ts // rc):
            rows = slice(c * rc, (c + 1) * rc)
            o_ref[rows, :] = attend(_dot(h_ref[rows, :], wq_ref[...]).astype(BF16), 0)
    else:
        q = _dot(h_ref[...], wq_ref[...]).astype(BF16)
        for s in range(nb):
            rows = slice(s * ts, (s + 1) * ts)
            o_ref[rows, :] = attend(q[rows, :], s)


def _q_attn(h, wq, mk, mv, layer, *, seq, heads, tm):
    m, k = h.shape
    hd = wq.shape[2] // heads
    mem_len = mk.shape[2]
    ts = min(tm, seq)
    nb = tm // ts
    tiles_per_seq = seq // ts
    kv_map = lambda i, hh: (layer, i // tiles_per_seq, 0, hh)
    return pl.pallas_call(
        functools.partial(_qattn_kernel, ts=ts, nb=nb, scale=float(hd) ** -0.5),
        grid=(m // tm, heads),
        in_specs=[
            pl.BlockSpec((tm, k), lambda i, hh: (i, 0)),
            pl.BlockSpec((None, k, hd), lambda i, hh: (layer, 0, hh)),
            pl.BlockSpec((None, nb, mem_len, hd), kv_map),
            pl.BlockSpec((None, nb, mem_len, hd), kv_map),
        ],
        out_specs=pl.BlockSpec((tm, hd), lambda i, hh: (i, hh)),
        out_shape=jax.ShapeDtypeStruct((m, heads * hd), BF16),
        compiler_params=_cparams(2),
        name="q_attn",
    )(h, wq, mk, mv)


def _mem_kv_kernel(mem_ref, g_ref, wk_ref, wv_ref, k_ref, v_ref, hn_scr):
    @pl.when(pl.program_id(2) == 0)
    def _():
        hn_scr[...] = (_rms(mem_ref[...]) * g_ref[...]).astype(BF16)

    hn = hn_scr[...]
    k_ref[...] = _dot(hn, wk_ref[...])
    v_ref[...] = _dot(hn, wv_ref[...])


def _mem_kv(mem, g, w_kv, *, tm, tn):
    m, d = mem.shape
    depth = w_kv.shape[0]
    n = w_kv.shape[2] // 2
    nj = n // tn
    return pl.pallas_call(
        _mem_kv_kernel,
        grid=(depth, m // tm, nj),
        in_specs=[
            pl.BlockSpec((tm, d), lambda l, i, j: (i, 0)),
            pl.BlockSpec((None, 1, d), lambda l, i, j: (l, 0, 0)),
            pl.BlockSpec((None, d, tn), lambda l, i, j: (l, 0, j)),
            pl.BlockSpec((None, d, tn), lambda l, i, j: (l, 0, j + nj)),
        ],
        out_specs=[pl.BlockSpec((None, tm, tn), lambda l, i, j: (l, i, j)),
                   pl.BlockSpec((None, tm, tn), lambda l, i, j: (l, i, j))],
        out_shape=[jax.ShapeDtypeStruct((depth, m, n), F32), jax.ShapeDtypeStruct((depth, m, n), F32)],
        scratch_shapes=[pltpu.VMEM((tm, d), BF16)],
        compiler_params=_cparams(3),
        name="mem_kv",
    )(mem, g, w_kv, w_kv)


OUT_COL_CHUNK = 1024
OUT_EPI_ROWS = 64
OUT_ROW_GROUP = 128


def _accumulate(lhs_ref, w_refs, acc_ref, first):
    lhs = lhs_ref[...]
    d = acc_ref.shape[1]
    wcols = d // len(w_refs)
    tn = min(wcols, OUT_COL_CHUNK)
    for c in range(d // tn):
        cs = slice(c * tn, (c + 1) * tn)
        w_ref = w_refs[(c * tn) // wcols]
        w0 = (c * tn) % wcols
        part = _dot(lhs, w_ref[:, w0:w0 + tn])
        acc_ref[:, cs] = part if first else acc_ref[:, cs] + part


def _res_norm_tail(acc_ref, x_ref, gpost_ref, gnext_ref, xo_ref, hn_ref, *, unrolled=False):
    gpost = gpost_ref[...]
    gnext = gnext_ref[...]
    rows = min(acc_ref.shape[0], OUT_EPI_ROWS)

    def chunk(sl):
        xn = x_ref[sl, :] + _rms(acc_ref[sl, :]) * gpost
        if xo_ref is not None:
            xo_ref[sl, :] = xn
        hn_ref[sl, :] = (_rms(xn) * gnext).astype(hn_ref.dtype)

    if unrolled:
        for c in range(acc_ref.shape[0] // rows):
            chunk(slice(c * rows, (c + 1) * rows))
    else:
        def body(c, carry):
            chunk(pl.ds(pl.multiple_of(c * rows, rows), rows))
            return carry

        lax.fori_loop(0, acc_ref.shape[0] // rows, body, 0)


def _outproj_ktiled_kernel(lhs_ref, wa_ref, wb_ref, x_hbm, gpost_ref, gnext_ref, *rest, nk, emit_x):
    w_ref = (wa_ref, wb_ref)
    if emit_x:
        xo_ref, hn_ref, x_buf, x_sem = rest
        acc_ref = xo_ref
    else:
        hn_ref, acc_ref, x_buf, x_sem = rest
        xo_ref = None
    tm = x_buf.shape[0]
    row0 = pl.multiple_of(pl.program_id(0) * tm, tm)
    x_copy = pltpu.make_async_copy(x_hbm.at[pl.ds(row0, tm), :], x_buf, x_sem)
    kk = pl.program_id(1)

    assert nk >= 2

    @pl.when(kk == 0)
    def _():
        x_copy.start()
        _accumulate(lhs_ref, w_ref, acc_ref, True)

    pl.when((kk > 0) & (kk < nk - 1))(functools.partial(_accumulate, lhs_ref, w_ref, acc_ref, False))

    @pl.when(kk == nk - 1)
    def _():
        x_copy.wait()
        group = min(tm, OUT_ROW_GROUP)
        for r in range(tm // group):
            rows = slice(r * group, (r + 1) * group)
            _accumulate(lhs_ref.at[rows], w_ref, acc_ref.at[rows], False)
            _res_norm_tail(acc_ref.at[rows], x_buf.at[rows], gpost_ref, gnext_ref,
                           None if xo_ref is None else xo_ref.at[rows], hn_ref.at[rows], unrolled=True)


def _outproj_resident_kernel(lhs_ref, w_ref, x_ref, gpost_ref, gnext_ref, xo_ref, hn_ref):
    tm = xo_ref.shape[0]
    group = min(tm, OUT_ROW_GROUP)
    for r in range(tm // group):
        rows = slice(r * group, (r + 1) * group)
        _accumulate(lhs_ref.at[rows], (w_ref,), xo_ref.at[rows], True)
        _res_norm_tail(xo_ref.at[rows], x_ref.at[rows], gpost_ref, gnext_ref, xo_ref.at[rows], hn_ref.at[rows],
                       unrolled=True)


def _outproj(lhs, w, layer, x, g_post, g_next, *, hn_dtype, emit_x, tm, tk):
    m, kdim = lhs.shape
    d = w.shape[2]
    out_shape = [jax.ShapeDtypeStruct((m, d), hn_dtype)]
    if emit_x:
        out_shape.insert(0, jax.ShapeDtypeStruct((m, d), F32))
    if tk is None:
        assert emit_x
        row_map = lambda i: (i, 0)
        fix_map = lambda i: (0, 0)
        return pl.pallas_call(
            _outproj_resident_kernel,
            grid=(m // tm,),
            in_specs=[
                pl.BlockSpec((tm, kdim), row_map),
                pl.BlockSpec((None, kdim, d), lambda i: (layer, 0, 0), pipeline_mode=pl.Buffered(1)),
                pl.BlockSpec((tm, d), row_map),
                pl.BlockSpec((1, d), fix_map),
                pl.BlockSpec((1, d), fix_map),
            ],
            out_specs=[pl.BlockSpec((tm, d), row_map), pl.BlockSpec((tm, d), row_map)],
            out_shape=out_shape,
            compiler_params=_cparams(1, V7X_VMEM_LIMIT_OUTPROJ_BYTES),
            name="outproj_resident",
        )(lhs, w, x, g_post, g_next)
    nk = kdim // tk
    row_map = lambda i, kk: (i, 0)
    fix_map = lambda i, kk: (0, 0)
    return pl.pallas_call(
        functools.partial(_outproj_ktiled_kernel, nk=nk, emit_x=emit_x),
        grid=(m // tm, nk),
        in_specs=[
            pl.BlockSpec((tm, tk), lambda i, kk: (i, kk)),
            pl.BlockSpec((None, tk, d // 2), lambda i, kk: (layer, kk, 0)),
            pl.BlockSpec((None, tk, d // 2), lambda i, kk: (layer, kk, 1)),
            pl.BlockSpec(memory_space=pl.ANY),
            pl.BlockSpec((1, d), fix_map),
            pl.BlockSpec((1, d), fix_map),
        ],
        out_specs=[pl.BlockSpec((tm, d), row_map)] * len(out_shape),
        out_shape=out_shape,
        scratch_shapes=([] if emit_x else [pltpu.VMEM((tm, d), F32)])
        + [pltpu.VMEM((tm, d), F32), pltpu.SemaphoreType.DMA(())],
        compiler_params=_cparams(2, V7X_VMEM_LIMIT_OUTPROJ_BYTES),
        name="outproj_ktiled",
    )(lhs, w, w, x, g_post, g_next)


def _pad_last(a, n):
    return jnp.pad(a, [(0, 0)] * (a.ndim - 1) + [(0, n - a.shape[-1])])


def _tiles(m):
    big = m >= 1024
    return dict(
        norm=256 if big else m,
        proj=1024 if big else m,
        sc=512 if big else m,
        attn=1024 if big else m,
        out=512 if big else m,
        out_resident=256 if big else m,
    )


def _sg_weights(w_s, b_s, seq):
    heads, ln, _ = w_s.shape
    pos = jnp.arange(ln) // CHUNK
    w = jnp.where(pos[None, :] <= pos[:, None], w_s, jnp.zeros_like(w_s))
    if seq % ln == 0:
        return w.astype(BF16), b_s.T
    assert ln % seq == 0
    reps = ln // seq
    blockdiag = jnp.kron(jnp.eye(reps, dtype=w.dtype), jnp.ones((seq, seq), w.dtype))
    wm = jnp.tile(w[:, :seq, :seq], (1, reps, reps)) * blockdiag
    return wm.astype(BF16), jnp.tile(b_s[:, :seq].T, (reps, 1))


def kernel(x_prompt, x_sample, mem_prompt, state_gla, cache_conv_c, cache_ffn_conv, cache_mem_k, cache_mem_v,
           norm_g, mem_norm_g, final_g,
           sg_w_in, sg_ln_g, sg_w_s, sg_b_s, sg_w_out,
           gla_w_in, gla_w_gate, gla_b_gate, gla_norm_g, gla_w_out,
           sc_w_in, sc_conv, sc_w_out,
           mem_w_q, mem_w_kv, mem_w_o,
           ffn_w_up, ffn_conv, ffn_w_down):
    depth = norm_g.shape[0]
    d_model = x_prompt.shape[-1]
    d_ff = ffn_conv.shape[-1]
    d_ffp = -(-d_ff // FFN_PAD) * FFN_PAD
    gla_heads, gla_dk, gla_dv = state_gla.shape[2], state_gla.shape[3], state_gla.shape[4]
    gla_main = 2 * gla_heads * (gla_dk + gla_dv)
    mem_heads = cache_mem_k.shape[3]
    mem_len = cache_mem_k.shape[2]
    row = lambda a: a.reshape(1, -1)

    def cast(w, **kw):
        kw.setdefault("tr", min(w.shape[1], CAST_TILE))
        kw.setdefault("tc", min(w.shape[2], CAST_TILE))
        return _cast_weights(w, **kw)

    w_sg_in, w_sg_out = cast(sg_w_in), cast(sg_w_out)
    w_gla_in = cast(gla_w_in, group_cols=gla_main)
    w_gla_lr = cast(gla_w_in, col_offset=gla_main, group_cols=gla_w_in.shape[2] - gla_main, out_group_cols=LANES,
                    tr=d_model, tc=LANES)
    w_gla_gate = [jnp.pad(gla_w_gate[b].astype(BF16), ((0, LANES - gla_w_gate.shape[1]), (0, 0)))
                  for b in range(gla_w_gate.shape[0])]
    w_gla_out = cast(gla_w_out)
    w_sc_in, w_sc_out = cast(sc_w_in), cast(sc_w_out)
    w_q, w_kv, w_o = cast(mem_w_q), cast(mem_w_kv), cast(mem_w_o)
    w_up = cast(ffn_w_up, groups=2, group_cols=d_ff, out_group_cols=d_ffp, tr=d_model, tc=FFN_CAST_TILE)
    w_down = cast(ffn_w_down, out_rows=d_ffp, tr=FFN_CAST_TILE, tc=d_model)
    ffn_cw = [_pad_last(ffn_conv[l], d_ffp) for l in range(depth)]

    def run(x, seq, mk, mv, s_gla, buf_c, buf_f, emit_v):
        m = x.shape[0]
        t = _tiles(m)
        sg_v, gla_out, c_out, f_out = [], [], [], []
        ia = ib = ic = 0
        hn = _rms_cast(x, row(norm_g[0, 0]), tm=t["norm"])
        for l in range(depth):
            kind = l % N_MIXERS
            if kind == 0:
                z = _proj(hn, w_sg_in, ia, act="gelu", tm=t["proj"], tn=PLAIN_TN)
                wm, bias = _sg_weights(sg_w_s[ia], sg_b_s[ia], seq)
                lhs, v_rows = _sg_mix(z, row(sg_ln_g[ia]), wm, bias, emit_v=emit_v)
                sg_v.append(v_rows)
                w_out, i_out = w_sg_out, ia
                ia += 1
            elif kind == 1:
                proj = _proj(hn, w_gla_in, ib, tm=t["proj"], tn=PLAIN_TN)
                glr = _proj(hn, w_gla_lr, ib, tm=t["proj"], tn=LANES)
                lhs, s_new = _gla_core(proj, glr, w_gla_gate[ib], row(gla_b_gate[ib]), row(gla_norm_g[ib]),
                                       s_gla[ib], seq=seq, heads=gla_heads)
                gla_out.append(s_new)
                w_out, i_out = w_gla_out, ib
                ib += 1
            else:
                lhs, cb = _proj_conv_gate(hn, w_sc_in, ic, sc_conv[ic], buf_c[ic], mode="sc", seq=seq,
                                          tm=t["sc"], tn=PROJ_TN)
                c_out.append(cb)
                w_out, i_out = w_sc_out, ic
                ic += 1
            x, hn = _outproj(lhs, w_out, i_out, x, row(norm_g[l, 1]), row(norm_g[l, 2]), hn_dtype=BF16, emit_x=True,
                             tm=t["out_resident"], tk=None)
            att = _q_attn(hn, w_q, mk, mv, l, seq=seq, heads=mem_heads, tm=t["attn"])
            x, hn = _outproj(att, w_o, l, x, row(norm_g[l, 3]), row(norm_g[l, 4]), hn_dtype=BF16, emit_x=True,
                             tm=t["out_resident"], tk=None)
            hid, fb = _proj_conv_gate(hn, w_up, l, ffn_cw[l], _pad_last(buf_f[l], d_ffp), mode="ffn", seq=seq,
                                      tm=t["proj"], tn=PROJ_TN)
            f_out.append(fb[:, :, :d_ff])
            if l < depth - 1:
                x, hn = _outproj(hid, w_down, l, x, row(norm_g[l, 5]), row(norm_g[l + 1, 0]), hn_dtype=BF16,
                                 emit_x=True, tm=t["out"], tk=OUT_TK)
            else:
                hn, = _outproj(hid, w_down, l, x, row(norm_g[l, 5]), row(final_g), hn_dtype=F32,
                               emit_x=False, tm=t["out"], tk=OUT_TK)
        return hn, sg_v, gla_out, c_out, f_out

    bp, sp, _ = x_prompt.shape
    bs, ss, _ = x_sample.shape

    mem2d = mem_prompt.reshape(bp * mem_len, d_model)
    mem_k_p, mem_v_p = _mem_kv(mem2d, mem_norm_g.reshape(depth, 1, d_model), w_kv, tm=MEM_KV_TILE, tn=MEM_KV_TILE)
    mem_k_p = mem_k_p.reshape(depth, bp, mem_len, d_model)
    mem_v_p = mem_v_p.reshape(depth, bp, mem_len, d_model)
    y_p, _, gla_p, conv_p, ffn_p = run(
        x_prompt.reshape(bp * sp, d_model), sp, mem_k_p, mem_v_p,
        jnp.zeros((state_gla.shape[0], bp) + state_gla.shape[2:], F32),
        jnp.zeros((cache_conv_c.shape[0], bp, 2, d_model), F32),
        jnp.zeros((depth, bp, 2, d_ff), F32), emit_v=False)

    y_s, sgv_s, gla_s, conv_s, ffn_s = run(
        x_sample.reshape(bs * ss, d_model), ss,
        cache_mem_k.reshape(depth, bs, mem_len, -1), cache_mem_v.reshape(depth, bs, mem_len, -1),
        state_gla, cache_conv_c, cache_ffn_conv, emit_v=True)

    kv_shape = (depth, bp, mem_len, mem_heads, d_model // mem_heads)
    return (y_p.reshape(bp, sp, d_model), y_s.reshape(bs, ss, d_model),
            jnp.stack(gla_p), jnp.stack(conv_p), jnp.stack(ffn_p),
            mem_k_p.reshape(kv_shape), mem_v_p.reshape(kv_shape),
            jnp.stack([v.reshape(bs, ss, -1) for v in sgv_s]), jnp.stack(gla_s), jnp.stack(conv_s), jnp.stack(ffn_s))
```
